```python
import functools
import jax, jax.numpy as jnp
from jax import lax
import numpy as np

D_MODEL = 1024
BATCH = 16
SEQ = 2048
DEPTH = 2
DEC_BATCH = 32
DEC_SEQ = 8
PAST_LEN = 16384
PAGE_SIZE = 128

W_A = 1024
CONV_A = 31
H_B = 16
KVH = 4
HD = 64
HPG = H_B // KVH
L_CMP = 32
L_SEL = 64
N_SEL = 16
WINDOW = 512
WIN_Q_BLK = 128
SEL_Q_BLK = 16
H_C = 8
DK_C = 128
DV_C = 128
HG_CHUNK = 16
D_FF = 2816
CONV_F = 3

EPS = 1e-6
NEG = -1e30
BIG = 1e4
LB_TINY = 1e-30
SCALE = HD ** -0.5
IN_WIDTHS = (2 * W_A, H_B * HD, 6 * KVH * HD, 3 * H_B, H_C * DK_C, H_C * DK_C, H_C * DV_C, H_C * DV_C, 3 * D_MODEL)
N_IN = sum(IN_WIDTHS)

kernel_name = 'hybrid_conformer_nsa_hgrn2_step'


def _rmsnorm(x, g):
    xf = x.astype(jnp.float32)
    y = xf * lax.rsqrt(jnp.mean(xf * xf, axis=-1, keepdims=True) + EPS)
    return (y * g.astype(jnp.float32)).astype(x.dtype)


def _layernorm(x, g, b):
    xf = x.astype(jnp.float32)
    xc = xf - jnp.mean(xf, axis=-1, keepdims=True)
    y = xc * lax.rsqrt(jnp.mean(xc * xc, axis=-1, keepdims=True) + EPS)
    return (y * g.astype(jnp.float32) + b.astype(jnp.float32)).astype(x.dtype)


def _dwconv(u, w):
    return lax.conv_general_dilated(u, w[:, None, :].astype(u.dtype), (1,), 'VALID',
                                    dimension_numbers=('NWC', 'WIO', 'NWC'),
                                    feature_group_count=u.shape[-1])


def _split_in(h):
    cuts = [int(c) for c in np.cumsum(IN_WIDTHS)[:-1]]
    return jnp.split(h, cuts, axis=-1)


def _conformer_branch(a_in, prev, conv_w, conv_b, ln_g, ln_b, w_out):
    a, gt = jnp.split(a_in, 2, axis=-1)
    u = a * jax.nn.sigmoid(gt)
    full = jnp.concatenate([prev.astype(u.dtype), u], axis=1)
    y = _dwconv(full, conv_w) + conv_b.astype(u.dtype)
    y = jax.nn.silu(_layernorm(y, ln_g, ln_b))
    return y @ w_out, full[:, full.shape[1] - (CONV_A - 1):]


def _compress(kv, pe, w):
    B, T, G, D = kv.shape
    nb = T // L_CMP
    blk = kv[:, :nb * L_CMP].reshape(B, nb, L_CMP, G, D) + pe[None, None, :, None, :].astype(kv.dtype)
    return jnp.einsum('bnlgd,lde->bnge', blk, w.astype(kv.dtype))


def _cmp_attention(q, q_pos, kc, vc):
    B, Tq = q.shape[:2]
    nb = kc.shape[1]
    qg = q.reshape(B, Tq, KVH, HPG, HD)
    s = jnp.einsum('btgjd,bngd->btgjn', qg, kc).astype(jnp.float32) * SCALE
    valid = ((jnp.arange(nb) + 1) * L_CMP - 1)[None, :] <= q_pos[:, None]
    mask = valid[None, :, None, None, :]
    p = jnp.where(mask, jax.nn.softmax(jnp.where(mask, s, NEG), axis=-1), 0.0)
    o = jnp.einsum('btgjn,bngd->btgjd', p.astype(vc.dtype), vc)
    return o.reshape(B, Tq, H_B, HD), p.sum(axis=3)


def _select_blocks(p_grp, q_pos, n_kv):
    ratio = L_SEL // L_CMP
    nsb = -(-n_kv // L_SEL)
    nb = p_grp.shape[-1]
    ps = jnp.pad(p_grp, ((0, 0), (0, 0), (0, 0), (0, nsb * ratio - nb)))
    ps = ps.reshape(p_grp.shape[:3] + (nsb, ratio)).sum(-1)
    j = jnp.arange(nsb)[None, :]
    cur = (q_pos // L_SEL)[:, None]
    valid = (j * L_SEL <= q_pos[:, None])[None, :, None, :]
    forced = ((j == 0) | (j == cur) | (j == cur - 1))[None, :, None, :]
    score = jnp.where(valid, jnp.where(forced, BIG, ps), -2.0 * BIG)
    _, idx = lax.top_k(score, min(N_SEL, nsb))
    return idx


def _sel_attention(q, q_pos, idx, gather):
    B, Tq = q.shape[:2]
    pos = idx[..., None] * L_SEL + jnp.arange(L_SEL)
    k, v = gather(pos)
    qg = q.reshape(B, Tq, KVH, HPG, HD)
    s = jnp.einsum('btgjd,btgnld->btgjnl', qg, k).astype(jnp.float32) * SCALE
    mask = (pos <= q_pos[None, :, None, None, None])[:, :, :, None]
    s = jnp.where(mask, s, NEG)
    p = jax.nn.softmax(s.reshape(s.shape[:4] + (-1,)), axis=-1).reshape(s.shape)
    o = jnp.einsum('btgjnl,btgnld->btgjd', p.astype(v.dtype), v)
    return o.reshape(B, Tq, H_B, HD)


def _win_attention(q, q_pos, k, v, k_pos):
    B, Tq = q.shape[:2]
    qg = q.reshape(B, Tq, KVH, HPG, HD)
    s = jnp.einsum('btgjd,bsgd->btgjs', qg, k).astype(jnp.float32) * SCALE
    d = q_pos[:, None] - k_pos[None, :]
    mask = ((d >= 0) & (d < WINDOW) & (k_pos >= 0)[None, :])[None, :, None, None, :]
    p = jax.nn.softmax(jnp.where(mask, s, NEG), axis=-1)
    o = jnp.einsum('btgjs,bsgd->btgjd', p.astype(v.dtype), v)
    return o.reshape(B, Tq, H_B, HD)


def _nsa_prompt(q, kv, pe, cw):
    B, T = q.shape[:2]
    pos = jnp.arange(T)
    kc = _compress(kv[:, :, 0], pe[0], cw[0])
    vc = _compress(kv[:, :, 1], pe[1], cw[1])
    o_cmp, p_grp = _cmp_attention(q, pos, kc, vc)
    idx = _select_blocks(p_grp, pos, T)
    n_top = idx.shape[-1]
    k_s, v_s = kv[:, :, 2], kv[:, :, 3]
    b_idx = jnp.arange(B)[:, None, None, None, None]
    g_idx = jnp.arange(KVH)[None, None, :, None, None]

    def gather(p):
        return k_s[b_idx, p, g_idx], v_s[b_idx, p, g_idx]

    nq = T // SEL_Q_BLK

    def sel_block(xs):
        q_blk, idx_blk, start = xs
        return _sel_attention(q_blk, start + jnp.arange(SEL_Q_BLK), idx_blk, gather)

    o_sel = lax.map(sel_block, (jnp.swapaxes(q.reshape(B, nq, SEL_Q_BLK, H_B, HD), 0, 1),
                                jnp.swapaxes(idx.reshape(B, nq, SEL_Q_BLK, KVH, n_top), 0, 1),
                                jnp.arange(nq) * SEL_Q_BLK))
    o_sel = jnp.swapaxes(o_sel, 0, 1).reshape(B, T, H_B, HD)

    k_w = jnp.pad(kv[:, :, 4], ((0, 0), (WINDOW, 0), (0, 0), (0, 0)))
    v_w = jnp.pad(kv[:, :, 5], ((0, 0), (WINDOW, 0), (0, 0), (0, 0)))
    nw = T // WIN_Q_BLK

    def win_block(xs):
        q_blk, start = xs
        k = lax.dynamic_slice_in_dim(k_w, start, WINDOW + WIN_Q_BLK, axis=1)
        v = lax.dynamic_slice_in_dim(v_w, start, WINDOW + WIN_Q_BLK, axis=1)
        k_pos = start - WINDOW + jnp.arange(WINDOW + WIN_Q_BLK)
        return _win_attention(q_blk, start + jnp.arange(WIN_Q_BLK), k, v, k_pos)

    o_win = lax.map(win_block, (jnp.swapaxes(q.reshape(B, nw, WIN_Q_BLK, H_B, HD), 0, 1),
                                jnp.arange(nw) * WIN_Q_BLK))
    o_win = jnp.swapaxes(o_win, 0, 1).reshape(B, T, H_B, HD)
    win_new = kv[:, T - min(WINDOW, T):, 4:6]
    return o_cmp, o_sel, o_win, kv[:, :, :4], win_new


def _nsa_sample(q, kv, pe, cw, cache, layer, page_table, win_prev):
    B, T = q.shape[:2]
    pos = PAST_LEN + jnp.arange(T)
    past = cache[layer, page_table, :, :2]
    past = past.reshape(B, -1, 2, KVH, HD)
    kc = jnp.concatenate([_compress(past[:, :, 0], pe[0], cw[0]),
                          _compress(kv[:, :, 0], pe[0], cw[0])], axis=1)
    vc = jnp.concatenate([_compress(past[:, :, 1], pe[1], cw[1]),
                          _compress(kv[:, :, 1], pe[1], cw[1])], axis=1)
    o_cmp, p_grp = _cmp_attention(q, pos, kc, vc)
    idx = _select_blocks(p_grp, pos, PAST_LEN + T)
    k_new, v_new = kv[:, :, 2], kv[:, :, 3]
    b_idx = jnp.arange(B)[:, None, None, None, None]
    g_idx = jnp.arange(KVH)[None, None, :, None, None]

    def gather(p):
        pc = jnp.minimum(p, PAST_LEN - 1)
        phys = page_table[b_idx, pc // PAGE_SIZE]
        off = pc % PAGE_SIZE
        pn = jnp.clip(p - PAST_LEN, 0, T - 1)
        is_past = (p < PAST_LEN)[..., None]
        k = jnp.where(is_past, cache[layer, phys, off, 2, g_idx], k_new[b_idx, pn, g_idx])
        v = jnp.where(is_past, cache[layer, phys, off, 3, g_idx], v_new[b_idx, pn, g_idx])
        return k, v

    o_sel = _sel_attention(q, pos, idx, gather)
    wb = win_prev.shape[1]
    win_all = jnp.concatenate([win_prev.astype(kv.dtype), kv[:, :, 4:6]], axis=1)
    k_pos = PAST_LEN - wb + jnp.arange(wb + T)
    o_win = _win_attention(q, pos, win_all[:, :, 0], win_all[:, :, 1], k_pos)
    return o_cmp, o_sel, o_win, kv[:, :, :4], win_all[:, T:]


def _gla_chunked(q, k, v, logf, s0):
    B, T, H, DK = q.shape
    DV = v.shape[-1]
    C = HG_CHUNK
    nc = -(-T // C)
    pad = nc * C - T

    def prep(a):
        a = jnp.pad(a, ((0, 0), (0, pad), (0, 0), (0, 0)))
        return a.reshape(B, nc, C, H, a.shape[-1]).transpose(1, 0, 3, 2, 4)

    causal = jnp.tril(jnp.ones((C, C), dtype=bool))[:, :, None]

    def step(S, xs):
        qc, kc, vc, lc = xs
        b = jnp.cumsum(lc, axis=2)
        o = jnp.einsum('bhtk,bhkv->bhtv', qc * jnp.exp(b), S)
        dec = jnp.exp(jnp.where(causal, b[:, :, :, None, :] - b[:, :, None, :, :], NEG))
        a = jnp.einsum('bhtk,bhsk,bhtsk->bhts', qc, kc, dec)
        o = o + jnp.einsum('bhts,bhsv->bhtv', a, vc)
        bl = b[:, :, -1:, :]
        S = jnp.exp(bl[:, :, 0, :, None]) * S + jnp.einsum('bhsk,bhsv->bhkv', kc * jnp.exp(bl - b), vc)
        return S, o

    S, o = lax.scan(step, s0, (prep(q), prep(k), prep(v), prep(logf)))
    o = o.transpose(1, 0, 3, 2, 4).reshape(B, nc * C, H, DV)[:, :T]
    return o, S


def _hgrn2_branch(c_q, c_f, c_i, c_g, s0, lb, norm_g, w_out):
    B, T = c_q.shape[:2]
    f32 = jnp.float32
    q = jax.nn.silu(c_q.astype(f32)).reshape(B, T, H_C, DK_C)
    fp = c_f.astype(f32).reshape(B, T, H_C, DK_C)
    lb = lb.reshape(H_C, DK_C)
    log_sig = jax.nn.log_sigmoid(fp)
    log_lb = jnp.log(jnp.maximum(lb, LB_TINY))
    logf = jnp.where(lb > 0, jnp.logaddexp(log_lb, jnp.log1p(-lb) + log_sig), log_sig)
    k = (1.0 - lb) * jax.nn.sigmoid(-fp)
    v = c_i.astype(f32).reshape(B, T, H_C, DV_C)
    o, s_new = _gla_chunked(q, k, v, logf, s0.astype(f32))
    o = _rmsnorm(o, norm_g.reshape(H_C, DV_C)).astype(c_g.dtype) * jax.nn.silu(c_g).reshape(B, T, H_C, DV_C)
    return o.reshape(B, T, H_C * DV_C) @ w_out, s_new.astype(s0.dtype)


def _conv_ffn(xn, prev, w_up, conv_w, w_down):
    u = xn @ w_up
    full = jnp.concatenate([prev.astype(u.dtype), u], axis=1)
    gate, val = jnp.split(_dwconv(full, conv_w), 2, axis=-1)
    return (jax.nn.silu(gate) * val) @ w_down, full[:, full.shape[1] - (CONV_F - 1):]


def _layer(x, lp, conv_a_prev, hg_prev, ffn_prev, nsa_fn):
    B, T = x.shape[:2]
    xn = _rmsnorm(x, lp['norm1_g'])
    a_in, b_q, b_kv, b_g, c_q, c_f, c_i, c_g, m_g = _split_in(xn @ lp['w_in'])
    y_a, conv_a_new = _conformer_branch(a_in, conv_a_prev, lp['conv_a_w'], lp['conv_a_b'],
                                        lp['ln_a_g'], lp['ln_a_b'], lp['w_a_out'])
    q = b_q.reshape(B, T, H_B, HD)
    kv = b_kv.reshape(B, T, 6, KVH, HD)
    o_cmp, o_sel, o_win, kv_rows, win_new = nsa_fn(q, kv)
    gb = jax.nn.sigmoid(b_g).reshape(B, T, H_B, 3, 1)
    o_b = gb[:, :, :, 0] * o_cmp + gb[:, :, :, 1] * o_sel + gb[:, :, :, 2] * o_win
    y_b = o_b.reshape(B, T, H_B * HD) @ lp['w_b_out']
    y_c, hg_new = _hgrn2_branch(c_q, c_f, c_i, c_g, hg_prev, lp['lb'], lp['hg_norm_g'], lp['w_c_out'])
    g_a, g_b, g_c = jnp.split(jax.nn.sigmoid(m_g), 3, axis=-1)
    x = x + (g_a * y_a + g_b * y_b + g_c * y_c) @ lp['w_out']
    y_f, ffn_new = _conv_ffn(_rmsnorm(x, lp['norm2_g']), ffn_prev, lp['w_up'], lp['conv_f_w'], lp['w_down'])
    x = x + y_f
    return x, kv_rows, win_new, conv_a_new, hg_new, ffn_new


def setup_inputs(seed: int = 0) -> dict:
    key = jax.random.key(seed)
    ks = jax.random.split(key, 32)
    f32 = jnp.float32

    def nrm(k, shape, s):
        return s * jax.random.normal(k, shape, f32)

    n_pages = PAST_LEN // PAGE_SIZE
    n_used = DEC_BATCH * n_pages
    n_pool = n_used + -(-n_used // 4)
    wb = min(WINDOW, PAST_LEN)
    page_table = jax.random.permutation(ks[3], n_pool)[:n_used].reshape(DEC_BATCH, n_pages).astype(jnp.int32)
    return {
        'x_prompt': nrm(ks[0], (BATCH, SEQ, D_MODEL), 1.0),
        'x_sample': nrm(ks[1], (DEC_BATCH, DEC_SEQ, D_MODEL), 1.0),
        'cache_nsa_kv': nrm(ks[2], (DEPTH, n_pool, PAGE_SIZE, 4, KVH, HD), 1.0),
        'page_table': page_table,
        'state_win_kv': nrm(ks[4], (DEPTH, DEC_BATCH, wb, 2, KVH, HD), 1.0),
        'state_conv_a': nrm(ks[5], (DEPTH, DEC_BATCH, CONV_A - 1, W_A), 0.5),
        'state_hgrn': nrm(ks[6], (DEPTH, DEC_BATCH, H_C, DK_C, DV_C), 0.3),
        'state_ffn_conv': nrm(ks[7], (DEPTH, DEC_BATCH, CONV_F - 1, 2 * D_FF), 1.0),
        'norm1_g': 1.0 + nrm(ks[8], (DEPTH, D_MODEL), 0.02),
        'w_in': nrm(ks[9], (DEPTH, D_MODEL, N_IN), D_MODEL ** -0.5),
        'conv_a_w': nrm(ks[10], (DEPTH, CONV_A, W_A), CONV_A ** -0.5),
        'conv_a_b': nrm(ks[11], (DEPTH, W_A), 0.02),
        'ln_a_g': 1.0 + nrm(ks[12], (DEPTH, W_A), 0.02),
        'ln_a_b': nrm(ks[13], (DEPTH, W_A), 0.02),
        'w_a_out': nrm(ks[14], (DEPTH, W_A, D_MODEL), W_A ** -0.5),
        'cmp_pe': nrm(ks[15], (DEPTH, 2, L_CMP, HD), 0.1),
        'cmp_w': nrm(ks[16], (DEPTH, 2, L_CMP, HD, HD), (L_CMP * HD) ** -0.5),
        'w_b_out': nrm(ks[17], (DEPTH, H_B * HD, D_MODEL), (H_B * HD) ** -0.5),
        'hg_lb_logits': nrm(ks[18], (DEPTH, H_C * DK_C), 0.5),
        'hg_norm_g': 1.0 + nrm(ks[19], (DEPTH, H_C * DV_C), 0.02),
        'w_c_out': nrm(ks[20], (DEPTH, H_C * DV_C, D_MODEL), (H_C * DV_C) ** -0.5),
        'w_out': nrm(ks[21], (DEPTH, D_MODEL, D_MODEL), D_MODEL ** -0.5),
        'norm2_g': 1.0 + nrm(ks[22], (DEPTH, D_MODEL), 0.02),
        'w_up': nrm(ks[23], (DEPTH, D_MODEL, 2 * D_FF), D_MODEL ** -0.5),
        'conv_f_w': nrm(ks[24], (DEPTH, CONV_F, 2 * D_FF), CONV_F ** -0.5),
        'w_down': nrm(ks[25], (DEPTH, D_FF, D_MODEL), D_FF ** -0.5),
        'final_g': 1.0 + nrm(ks[26], (D_MODEL,), 0.02),
    }


def reference(x_prompt, x_sample, cache_nsa_kv, page_table, state_win_kv, state_conv_a, state_hgrn,
              state_ffn_conv, norm1_g, w_in, conv_a_w, conv_a_b, ln_a_g, ln_a_b, w_a_out, cmp_pe, cmp_w,
              w_b_out, hg_lb_logits, hg_norm_g, w_c_out, w_out, norm2_g, w_up, conv_f_w, w_down, final_g):
    lb_p = jax.nn.softmax(hg_lb_logits.astype(jnp.float32), axis=0)
    lb_all = jnp.cumsum(lb_p, axis=0) - lb_p
    bp = x_prompt.shape[0]
    xp, xs = x_prompt, x_sample
    kv_p, kv_s, win_p, win_s, ca_p, ca_s, hg_p, hg_s, ff_p, ff_s = [], [], [], [], [], [], [], [], [], []
    for l in range(DEPTH):
        lp = {'norm1_g': norm1_g[l], 'w_in': w_in[l], 'conv_a_w': conv_a_w[l], 'conv_a_b': conv_a_b[l],
              'ln_a_g': ln_a_g[l], 'ln_a_b': ln_a_b[l], 'w_a_out': w_a_out[l], 'w_b_out': w_b_out[l],
              'lb': lb_all[l], 'hg_norm_g': hg_norm_g[l], 'w_c_out': w_c_out[l], 'w_out': w_out[l],
              'norm2_g': norm2_g[l], 'w_up': w_up[l], 'conv_f_w': conv_f_w[l], 'w_down': w_down[l]}
        nsa_p = functools.partial(_nsa_prompt, pe=cmp_pe[l], cw=cmp_w[l])
        nsa_s = functools.partial(_nsa_sample, pe=cmp_pe[l], cw=cmp_w[l], cache=cache_nsa_kv, layer=l,
                                  page_table=page_table, win_prev=state_win_kv[l])
        xp, a1, a2, a3, a4, a5 = _layer(xp, lp,
                                        jnp.zeros((bp, CONV_A - 1, W_A), xp.dtype),
                                        jnp.zeros((bp, H_C, DK_C, DV_C), xp.dtype),
                                        jnp.zeros((bp, CONV_F - 1, 2 * D_FF), xp.dtype), nsa_p)
        kv_p.append(a1); win_p.append(a2); ca_p.append(a3); hg_p.append(a4); ff_p.append(a5)
        xs, s1, s2, s3, s4, s5 = _layer(xs, lp, state_conv_a[l], state_hgrn[l], state_ffn_conv[l], nsa_s)
        kv_s.append(s1); win_s.append(s2); ca_s.append(s3); hg_s.append(s4); ff_s.append(s5)
    y_prompt = _rmsnorm(xp, final_g)
    y_sample = _rmsnorm(xs, final_g)
    return (y_prompt, y_sample, jnp.stack(kv_p), jnp.stack(kv_s), jnp.stack(win_p), jnp.stack(win_s),
            jnp.stack(ca_p), jnp.stack(ca_s), jnp.stack(hg_p), jnp.stack(hg_s), jnp.stack(ff_p), jnp.stack(ff_s))
```

```python
import functools
import math

import jax
import jax.numpy as jnp
import numpy as np
from jax import lax
from jax.experimental import pallas as pl
from jax.experimental.pallas import tpu as pltpu

F32 = jnp.float32
BF16 = jnp.bfloat16

H_B = 16
KVH = 4
HD = 64
HPG = H_B // KVH
L_CMP = 32
L_SEL = 64
N_SEL = 16
WINDOW = 512
H_C = 8
DK_C = 128
DV_C = 128
EPS = 1e-6
NEG = -1e30
BIG = 1e4
LB_TINY = 1e-30
SCALE = HD ** -0.5

LANE = 128
VMEM_LIMIT = 56 * 1024 * 1024

CONV_HALO = 32
FFN_HALO = 8
PAGES_PER_STEP = 8
NSB_PAD = 128


def _cp(sem, vmem=VMEM_LIMIT):
    return pltpu.CompilerParams(dimension_semantics=sem, vmem_limit_bytes=vmem)


def _tile(n, pref):
    t = min(n, pref)
    while n % t:
        t -= 8
    assert t > 0
    return t


def _sigmoid(x):
    return 1.0 / (1.0 + jnp.exp(-x))


W_A = 1024
OFF_A = 0
OFF_GT = OFF_A + W_A
OFF_Q = OFF_GT + W_A
OFF_MG = OFF_Q + H_B * HD
OFF_CQ = OFF_MG + 3 * 1024
OFF_CF = OFF_CQ + H_C * DK_C
OFF_CI = OFF_CF + H_C * DK_C
OFF_CG = OFF_CI + H_C * DV_C
OFF_KV = OFF_CG + H_C * DV_C
OFF_BG = OFF_KV + 6 * KVH * HD
N_PACK = OFF_BG + KVH * LANE
IN_WIDTHS = (2 * W_A, H_B * HD, 6 * KVH * HD, 3 * H_B, H_C * DK_C, H_C * DK_C, H_C * DV_C, H_C * DV_C, 3 * 1024)


def _pack_w_in(w):
    d = w.shape[0]
    cuts = [int(c) for c in np.cumsum(IN_WIDTHS)[:-1]]
    a_in, b_q, b_kv, b_g, c_q, c_f, c_i, c_g, m_g = jnp.split(w, cuts, axis=1)
    bg = b_g.reshape(d, KVH, HPG * 3)
    bg = jnp.pad(bg, ((0, 0), (0, 0), (0, LANE - HPG * 3))).reshape(d, KVH * LANE)
    return jnp.concatenate([a_in, b_q, m_g, c_q, c_f, c_i, c_g, b_kv, bg], axis=1).astype(BF16)


def _rmsnorm_kernel(x_ref, g_ref, o_ref):
    x = x_ref[...]
    y = x * lax.rsqrt(jnp.mean(x * x, axis=-1, keepdims=True) + EPS)
    o_ref[...] = (y * g_ref[...]).astype(o_ref.dtype)


def _rmsnorm(x2d, g, out_dtype):
    m, d = x2d.shape
    tm = _tile(m, 512)
    return pl.pallas_call(
        _rmsnorm_kernel, grid=(m // tm,),
        in_specs=[pl.BlockSpec((tm, d), lambda i: (i, 0)), pl.BlockSpec((1, d), lambda i: (0, 0))],
        out_specs=pl.BlockSpec((tm, d), lambda i: (i, 0)),
        out_shape=jax.ShapeDtypeStruct((m, d), out_dtype),
        compiler_params=_cp(("parallel",)), name="rmsnorm")(x2d, g.reshape(1, d))


def _matmul_kernel(x_ref, w_ref, o_ref):
    o_ref[...] = jnp.dot(x_ref[...], w_ref[...], preferred_element_type=F32)


def _matmul(x, w, tn, name):
    m, k = x.shape
    n = w.shape[1]
    tm = _tile(m, 512)
    assert n % tn == 0
    return pl.pallas_call(
        _matmul_kernel, grid=(n // tn, m // tm),
        in_specs=[pl.BlockSpec((tm, k), lambda j, i: (i, 0)), pl.BlockSpec((k, tn), lambda j, i: (0, j))],
        out_specs=pl.BlockSpec((tm, tn), lambda j, i: (i, j)),
        out_shape=jax.ShapeDtypeStruct((m, n), F32),
        compiler_params=_cp(("parallel", "parallel")), name=name)(x, w)


def _conformer_kernel(a_ref, g_ref, ah_ref, gh_ref, prev_ref, w_ref, b_ref, lg_ref, lb_ref,
                      z_ref, st_ref, full_ref, y_ref, *, ktaps, tt):
    t = pl.program_id(1)
    u_halo = ah_ref[...] * _sigmoid(gh_ref[...])
    full_ref[0:CONV_HALO, :] = jnp.where(t == 0, prev_ref[...], u_halo)
    full_ref[CONV_HALO:CONV_HALO + tt, :] = a_ref[...] * _sigmoid(g_ref[...])
    width = y_ref.shape[1]

    def lane_chunk(c, carry):
        cs = pl.ds(pl.multiple_of(c * LANE, LANE), LANE)
        acc = jnp.broadcast_to(b_ref[:, cs], (tt, LANE))
        for j in range(ktaps):
            acc = acc + w_ref[j:j + 1, cs] * full_ref[pl.ds(CONV_HALO - (ktaps - 1) + j, tt), cs]
        y_ref[:, cs] = acc
        return carry

    lax.fori_loop(0, width // LANE, lane_chunk, 0)
    y = y_ref[...]
    xc = y - jnp.mean(y, axis=-1, keepdims=True)
    yn = xc * lax.rsqrt(jnp.mean(xc * xc, axis=-1, keepdims=True) + EPS) * lg_ref[...] + lb_ref[...]
    z_ref[...] = (yn * _sigmoid(yn)).astype(z_ref.dtype)
    st_ref[...] = full_ref[tt:tt + CONV_HALO, :]


def _conformer(h3, prev, conv_w, conv_b, ln_g, ln_b):
    b, t, _ = h3.shape
    ktaps = conv_w.shape[0]
    assert ktaps - 1 <= CONV_HALO
    tt = _tile(t, 256)
    nt = t // tt
    if t >= CONV_HALO:
        assert tt % CONV_HALO == 0
        halo_src = h3
        ratio = tt // CONV_HALO
        halo_a = lambda i, j: (i, jnp.maximum(j * ratio - 1, 0), OFF_A // W_A)
        halo_g = lambda i, j: (i, jnp.maximum(j * ratio - 1, 0), OFF_GT // W_A)
    else:
        assert nt == 1
        halo_src = jnp.zeros((b, CONV_HALO, W_A), F32)
        halo_a = lambda i, j: (i, 0, 0)
        halo_g = halo_a
    row = lambda v: v.reshape(1, W_A)
    z, st = pl.pallas_call(
        functools.partial(_conformer_kernel, ktaps=ktaps, tt=tt), grid=(b, nt),
        in_specs=[pl.BlockSpec((None, tt, W_A), lambda i, j: (i, j, OFF_A // W_A)),
                  pl.BlockSpec((None, tt, W_A), lambda i, j: (i, j, OFF_GT // W_A)),
                  pl.BlockSpec((None, CONV_HALO, W_A), halo_a),
                  pl.BlockSpec((None, CONV_HALO, W_A), halo_g),
                  pl.BlockSpec((None, CONV_HALO, W_A), lambda i, j: (i, 0, 0)),
                  pl.BlockSpec((ktaps, W_A), lambda i, j: (0, 0)),
                  pl.BlockSpec((1, W_A), lambda i, j: (0, 0)),
                  pl.BlockSpec((1, W_A), lambda i, j: (0, 0)),
                  pl.BlockSpec((1, W_A), lambda i, j: (0, 0))],
        out_specs=[pl.BlockSpec((None, tt, W_A), lambda i, j: (i, j, 0)),
                   pl.BlockSpec((None, CONV_HALO, W_A), lambda i, j: (i, 0, 0))],
        out_shape=[jax.ShapeDtypeStruct((b, t, W_A), BF16), jax.ShapeDtypeStruct((b, CONV_HALO, W_A), F32)],
        scratch_shapes=[pltpu.VMEM((CONV_HALO + tt, W_A), F32), pltpu.VMEM((tt, W_A), F32)],
        compiler_params=_cp(("parallel", "arbitrary")), name="conformer",
    )(h3, h3, halo_src, halo_src, prev, conv_w, row(conv_b), row(ln_g), row(ln_b))
    return z, st[:, CONV_HALO - (ktaps - 1):]


def _blockdiag_cmp_w(cw):
    eye = jnp.eye(KVH, dtype=cw.dtype)
    bd = jnp.einsum('klde,gh->klgdhe', cw, eye)
    return bd.reshape(2, L_CMP * KVH * HD, KVH * HD).astype(BF16)


def _tile_pe(pe):
    return jnp.broadcast_to(pe[:, :, None, :], (2, L_CMP, KVH, HD)).reshape(2, 1, L_CMP * KVH * HD)


def _compress_kernel(x_ref, pe_ref, w_ref, o_ref):
    x = (x_ref[...] + pe_ref[...]).astype(BF16)
    o_ref[...] = jnp.dot(x, w_ref[...], preferred_element_type=F32)


def _compress(x, pe_t, bd):
    _, r, kdim = x.shape
    n = KVH * HD
    tr = _tile(r, 256)
    return pl.pallas_call(
        _compress_kernel, grid=(2, r // tr),
        in_specs=[pl.BlockSpec((None, tr, kdim), lambda s, i: (s, i, 0)),
                  pl.BlockSpec((None, 1, kdim), lambda s, i: (s, 0, 0)),
                  pl.BlockSpec((None, kdim, n), lambda s, i: (s, 0, 0))],
        out_specs=pl.BlockSpec((None, tr, n), lambda s, i: (s, i, 0)),
        out_shape=jax.ShapeDtypeStruct((2, r, n), F32),
        compiler_params=_cp(("parallel", "parallel")), name="nsa_compress")(x, pe_t, bd)


def _even_odd(x, axis):
    ev = lax.slice_in_dim(x, 0, x.shape[axis], 2, axis)
    od = lax.slice_in_dim(x, 1, x.shape[axis], 2, axis)
    return jnp.concatenate([ev, od], axis=axis)


def _topk_rows(score, k):
    n = score.shape[0]
    ridx = lax.broadcasted_iota(jnp.int32, score.shape, 0).astype(F32)
    sel = jnp.zeros(score.shape, F32)
    s = score
    for _ in range(k):
        m = jnp.max(s, axis=0, keepdims=True)
        imin = jnp.min(jnp.where(s == m, ridx, float(n)), axis=0, keepdims=True)
        hit = ridx == imin
        sel = jnp.where(hit, 1.0, sel)
        s = jnp.where(hit, -jnp.inf, s)
    return sel


def _topk_lanes(score, k):
    n = score.shape[-1]
    cidx = lax.broadcasted_iota(jnp.int32, score.shape, score.ndim - 1).astype(F32)
    sel = jnp.zeros(score.shape, F32)
    s = score
    for _ in range(k):
        m = jnp.max(s, axis=-1, keepdims=True)
        imin = jnp.min(jnp.where(s == m, cidx, float(n)), axis=-1, keepdims=True)
        hit = cidx == imin
        sel = jnp.where(hit, 1.0, sel)
        s = jnp.where(hit, -jnp.inf, s)
    return sel


def _flash_step(j, s, vs, m_sc, l_sc, acc_sc):
    m_prev = m_sc[j]
    m_new = jnp.maximum(m_prev, jnp.max(s, axis=1, keepdims=True))
    alpha = jnp.exp(m_prev - m_new)
    p = jnp.exp(s - m_new[:, 0:1])
    l_sc[j] = alpha * l_sc[j] + jnp.sum(p, axis=1, keepdims=True)
    acc_sc[j] = acc_sc[j] * alpha[:, 0:HD] + jnp.dot(p.astype(BF16), vs, preferred_element_type=F32)
    m_sc[j] = m_new


def _flash_reset(m_sc, l_sc, acc_sc):
    m_sc[...] = jnp.full(m_sc.shape, NEG, F32)
    l_sc[...] = jnp.zeros(l_sc.shape, F32)
    acc_sc[...] = jnp.zeros(acc_sc.shape, F32)


_NT = (((1,), (1,)), ((), ()))
_TN = (((0,), (0,)), ((), ()))


def _nsa_prompt_kernel(q_ref, bg_ref, kc_ref, vc_ref, ks_ref, vs_ref, kw_ref, vw_ref, o_ref,
                       q_sc, selm_sc, m_sc, l_sc, acc_sc, *, tq, tk, nb, nsb):
    t0 = pl.program_id(2) * tq
    q = (q_ref[...] * SCALE).astype(BF16)
    for j in range(HPG):
        q_sc[j] = q[:, j * HD:(j + 1) * HD]
    kc = kc_ref[...]
    vc = vc_ref[...]

    half = nb // 2
    row = lax.broadcasted_iota(jnp.int32, (nb, tq), 0)
    tpos = t0 + lax.broadcasted_iota(jnp.int32, (nb, tq), 1)
    blk = jnp.where(row < half, 2 * row, 2 * (row - half) + 1)
    cvalid = (blk + 1) * L_CMP - 1 <= tpos
    pg = jnp.zeros((nb, tq), F32)
    o_cmp = []
    for j in range(HPG):
        st = lax.dot_general(kc, q_sc[j], _NT, preferred_element_type=F32)
        sm = jnp.where(cvalid, st, NEG)
        e = jnp.exp(sm - jnp.max(sm, axis=0, keepdims=True))
        p = jnp.where(cvalid, e / jnp.sum(e, axis=0, keepdims=True), 0.0)
        pg = pg + p
        o_cmp.append(lax.dot_general(p.astype(BF16), vc, _TN, preferred_element_type=F32))

    ps = pg[:half] + pg[half:]
    jrow = lax.broadcasted_iota(jnp.int32, (nsb, tq), 0)
    tp = t0 + lax.broadcasted_iota(jnp.int32, (nsb, tq), 1)
    cur = tp // L_SEL
    forced_ps = jnp.where(jrow == 0, BIG, jnp.where(jrow == cur, BIG, jnp.where(jrow == cur - 1, BIG, ps)))
    score = jnp.where(jrow * L_SEL <= tp, forced_ps, -2.0 * BIG)
    sel = _topk_rows(score, min(N_SEL, nsb))
    sel = jnp.concatenate([sel, jnp.zeros((NSB_PAD - nsb, tq), F32)], axis=0)
    selm_sc[...] = jnp.transpose(sel).astype(BF16)

    c_hi = (t0 + tq + tk - 1) // tk
    qpos = t0 + lax.broadcasted_iota(jnp.int32, (tq, tk), 0)
    kofs = lax.broadcasted_iota(jnp.int32, (tq, tk), 1)

    _flash_reset(m_sc, l_sc, acc_sc)

    def sel_chunk(c, carry):
        ks = ks_ref[pl.ds(pl.multiple_of(c * tk, tk), tk), :]
        vs = vs_ref[pl.ds(pl.multiple_of(c * tk, tk), tk), :]
        jb = lax.broadcasted_iota(jnp.int32, (NSB_PAD, tk), 0)
        kb = (c * tk + lax.broadcasted_iota(jnp.int32, (NSB_PAD, tk), 1)) // L_SEL
        expand = jnp.where(jb == kb, 1.0, 0.0).astype(BF16)
        chosen = jnp.dot(selm_sc[...], expand, preferred_element_type=F32) > 0.5
        causal = c * tk + kofs <= qpos
        for j in range(HPG):
            s = lax.dot_general(q_sc[j], ks, _NT, preferred_element_type=F32)
            s = jnp.where(causal, jnp.where(chosen, s, NEG), NEG)
            _flash_step(j, s, vs, m_sc, l_sc, acc_sc)
        return carry

    lax.fori_loop(0, c_hi, sel_chunk, 0)
    o_sel = [acc_sc[j] / l_sc[j][:, 0:HD] for j in range(HPG)]

    _flash_reset(m_sc, l_sc, acc_sc)

    def win_chunk(c, carry):
        ks = kw_ref[pl.ds(pl.multiple_of(c * tk, tk), tk), :]
        vs = vw_ref[pl.ds(pl.multiple_of(c * tk, tk), tk), :]
        d = qpos - (c * tk + kofs)
        for j in range(HPG):
            s = lax.dot_general(q_sc[j], ks, _NT, preferred_element_type=F32)
            s = jnp.where(d >= 0, jnp.where(d < WINDOW, s, NEG), NEG)
            _flash_step(j, s, vs, m_sc, l_sc, acc_sc)
        return carry

    lax.fori_loop(jnp.maximum(t0 - (WINDOW - 1), 0) // tk, c_hi, win_chunk, 0)

    sg = _sigmoid(bg_ref[...])
    outs = []
    for j in range(HPG):
        o_win = acc_sc[j] / l_sc[j][:, 0:HD]
        outs.append(sg[:, 3 * j:3 * j + 1] * o_cmp[j] + sg[:, 3 * j + 1:3 * j + 2] * o_sel[j]
                    + sg[:, 3 * j + 2:3 * j + 3] * o_win)
    o_ref[...] = jnp.concatenate(outs, axis=1).astype(o_ref.dtype)


def _nsa_prompt(h3, pe_t, bd):
    b, t, _ = h3.shape
    tq, tk = 128, 256
    assert t % tk == 0 and t % L_SEL == 0
    nb = t // L_CMP
    nsb = t // L_SEL
    assert nsb <= NSB_PAD and nb == 2 * nsb
    kv = h3[:, :, OFF_KV:OFF_KV + 6 * KVH * HD]
    xc = jnp.stack([kv[:, :, 0:KVH * HD].reshape(b * nb, L_CMP * KVH * HD),
                    kv[:, :, KVH * HD:2 * KVH * HD].reshape(b * nb, L_CMP * KVH * HD)])
    cmp = _compress(xc, pe_t, bd).reshape(2, b, nb, KVH, HD)
    cmp = jnp.transpose(_even_odd(cmp, 2), (0, 1, 3, 2, 4)).astype(BF16)
    kvh = jnp.transpose(kv[:, :, 2 * KVH * HD:].reshape(b, t, 4, KVH, HD), (2, 0, 3, 1, 4)).astype(BF16)
    qcol = OFF_Q // (HPG * HD)
    bgcol = OFF_BG // LANE
    head = lambda: pl.BlockSpec((None, None, t, HD), lambda i, g, n: (i, g, 0, 0))
    return pl.pallas_call(
        functools.partial(_nsa_prompt_kernel, tq=tq, tk=tk, nb=nb, nsb=nsb), grid=(b, KVH, t // tq),
        in_specs=[pl.BlockSpec((None, tq, HPG * HD), lambda i, g, n: (i, n, qcol + g)),
                  pl.BlockSpec((None, tq, LANE), lambda i, g, n: (i, n, bgcol + g)),
                  pl.BlockSpec((None, None, nb, HD), lambda i, g, n: (i, g, 0, 0)),
                  pl.BlockSpec((None, None, nb, HD), lambda i, g, n: (i, g, 0, 0)),
                  head(), head(), head(), head()],
        out_specs=pl.BlockSpec((None, tq, HPG * HD), lambda i, g, n: (i, n, g)),
        out_shape=jax.ShapeDtypeStruct((b, t, H_B * HD), BF16),
        scratch_shapes=[pltpu.VMEM((HPG, tq, HD), BF16), pltpu.VMEM((tq, NSB_PAD), BF16),
                        pltpu.VMEM((HPG, tq, LANE), F32), pltpu.VMEM((HPG, tq, LANE), F32),
                        pltpu.VMEM((HPG, tq, HD), F32)],
        compiler_params=_cp(("parallel", "parallel", "arbitrary")), name="nsa_prompt",
    )(h3, h3, cmp[0], cmp[1], kvh[0], kvh[1], kvh[2], kvh[3])


def _cmp_past_kernel(pt_ref, *refs, n_steps, rows):
    pages = refs[:PAGES_PER_STEP]
    pe_ref, w_ref, o_ref, x_sc = refs[PAGES_PER_STEP:]
    pg = pl.program_id(2)
    x = jnp.concatenate([p[...] for p in pages], axis=0)
    x_sc[pl.ds(pl.multiple_of(pg * rows, rows), rows), :] = (x + pe_ref[...]).astype(BF16)

    @pl.when(pg == n_steps - 1)
    def _():
        o_ref[...] = jnp.dot(x_sc[...], w_ref[...], preferred_element_type=F32)


def _cmp_past(cache_cmp, page_table, layer, pe_t, bd):
    b, n_pages = page_table.shape
    per_page, kdim = cache_cmp.shape[3], cache_cmp.shape[4]
    assert n_pages % PAGES_PER_STEP == 0
    n_steps = n_pages // PAGES_PER_STEP
    rows = PAGES_PER_STEP * per_page
    assert rows % 16 == 0
    nbp = n_pages * per_page
    n = KVH * HD
    specs = []
    for slot in range(PAGES_PER_STEP):
        def index_map(i, s, pg, pt, slot=slot):
            return (layer, s, pt[i, pg * PAGES_PER_STEP + slot], 0, 0)
        specs.append(pl.BlockSpec((None, None, None, per_page, kdim), index_map))
    grid_spec = pltpu.PrefetchScalarGridSpec(
        num_scalar_prefetch=1, grid=(b, 2, n_steps),
        in_specs=specs + [pl.BlockSpec((None, 1, kdim), lambda i, s, pg, pt: (s, 0, 0)),
                          pl.BlockSpec((None, kdim, n), lambda i, s, pg, pt: (s, 0, 0))],
        out_specs=pl.BlockSpec((None, None, nbp, n), lambda i, s, pg, pt: (i, s, 0, 0)),
        scratch_shapes=[pltpu.VMEM((nbp, kdim), BF16)])
    return pl.pallas_call(
        functools.partial(_cmp_past_kernel, n_steps=n_steps, rows=rows), grid_spec=grid_spec,
        out_shape=jax.ShapeDtypeStruct((b, 2, nbp, n), F32),
        compiler_params=_cp(("parallel", "parallel", "arbitrary")), name="nsa_cmp_past",
    )(page_table, *([cache_cmp] * PAGES_PER_STEP), pe_t, bd)


def _stack_heads(x, g):
    return jnp.concatenate([x[:, (g * HPG + j) * HD:(g * HPG + j + 1) * HD] for j in range(HPG)], axis=0)


def _sample_cmp_win_kernel(q_ref, bg_ref, kc_ref, vc_ref, wp_ref, wn_ref, ocw_ref, sel_ref,
                           *, t, past, nbp, nsb, sel_pad):
    q = (q_ref[...] * SCALE).astype(BF16)
    sg = _sigmoid(bg_ref[...])
    rows = HPG * t
    half = nbp // 2
    wb = wp_ref.shape[0]
    gw = KVH * HD

    col = lax.broadcasted_iota(jnp.int32, (rows, nbp), 1)
    tq = lax.broadcasted_iota(jnp.int32, (rows, nbp), 0) % t
    blk = jnp.where(col < half, 2 * col, 2 * (col - half) + 1)
    cvalid = (blk + 1) * L_CMP - 1 <= past + tq

    wi = lax.broadcasted_iota(jnp.int32, (rows, wb), 1)
    wt = lax.broadcasted_iota(jnp.int32, (rows, wb), 0) % t
    d_prev = wt + wb - wi
    ni = lax.broadcasted_iota(jnp.int32, (rows, t), 1)
    nt = lax.broadcasted_iota(jnp.int32, (rows, t), 0) % t
    d_new = nt - ni

    ps_all = []
    outs = [None] * H_B
    for g in range(KVH):
        qg = _stack_heads(q, g)
        kc = kc_ref[:, g * HD:(g + 1) * HD].astype(BF16)
        vc = vc_ref[:, g * HD:(g + 1) * HD].astype(BF16)
        s = lax.dot_general(qg, kc, _NT, preferred_element_type=F32)
        sm = jnp.where(cvalid, s, NEG)
        e = jnp.exp(sm - jnp.max(sm, axis=1, keepdims=True))
        p = jnp.where(cvalid, e / jnp.sum(e, axis=1, keepdims=True), 0.0)
        o_cmp = jnp.dot(p.astype(BF16), vc, preferred_element_type=F32)
        pgrp = p[0:t]
        for j in range(1, HPG):
            pgrp = pgrp + p[j * t:(j + 1) * t]
        ps_all.append(pgrp[:, :half] + pgrp[:, half:])

        kp = wp_ref[:, g * HD:(g + 1) * HD].astype(BF16)
        vp = wp_ref[:, gw + g * HD:gw + (g + 1) * HD].astype(BF16)
        kn = wn_ref[:, g * HD:(g + 1) * HD].astype(BF16)
        vn = wn_ref[:, gw + g * HD:gw + (g + 1) * HD].astype(BF16)
        s1 = lax.dot_general(qg, kp, _NT, preferred_element_type=F32)
        s1 = jnp.where(d_prev >= 0, jnp.where(d_prev < WINDOW, s1, NEG), NEG)
        s2 = lax.dot_general(qg, kn, _NT, preferred_element_type=F32)
        s2 = jnp.where(d_new >= 0, jnp.where(d_new < WINDOW, s2, NEG), NEG)
        mx = jnp.maximum(jnp.max(s1, axis=1, keepdims=True), jnp.max(s2, axis=1, keepdims=True))
        e1 = jnp.exp(s1 - mx)
        e2 = jnp.exp(s2 - mx)
        den = jnp.sum(e1, axis=1, keepdims=True) + jnp.sum(e2, axis=1, keepdims=True)
        o_win = (jnp.dot(e1.astype(BF16), vp, preferred_element_type=F32)
                 + jnp.dot(e2.astype(BF16), vn, preferred_element_type=F32)) / den
        for j in range(HPG):
            c = g * LANE + 3 * j
            outs[g * HPG + j] = (sg[:, c:c + 1] * o_cmp[j * t:(j + 1) * t]
                                 + sg[:, c + 2:c + 3] * o_win[j * t:(j + 1) * t])
    ocw_ref[...] = jnp.concatenate(outs, axis=1)

    ps = jnp.concatenate(ps_all, axis=0)
    ps = jnp.concatenate([ps, jnp.zeros((KVH * t, sel_pad - half), F32)], axis=1)
    jb = lax.broadcasted_iota(jnp.int32, (KVH * t, sel_pad), 1)
    qp = past + lax.broadcasted_iota(jnp.int32, (KVH * t, sel_pad), 0) % t
    cur = qp // L_SEL
    forced_ps = jnp.where(jb == 0, BIG, jnp.where(jb == cur, BIG, jnp.where(jb == cur - 1, BIG, ps)))
    score = jnp.where(jb * L_SEL <= qp, forced_ps, -2.0 * BIG)
    score = jnp.where(jb < nsb, score, -jnp.inf)
    sel_ref[...] = _topk_lanes(score, min(N_SEL, nsb))


def _sample_sel_kernel(pt_ref, *refs, t, page, n_steps, n_past_blk, sel_pad):
    pages = refs[:PAGES_PER_STEP]
    q_ref, bg_ref, sel_ref, kvn_ref, ocw_ref, o_ref, q_sc, m_sc, l_sc, acc_sc = refs[PAGES_PER_STEP:]
    pg = pl.program_id(1)
    rows = HPG * t
    gw = KVH * HD
    keys = PAGES_PER_STEP * page

    @pl.when(pg == 0)
    def _():
        q = (q_ref[...] * SCALE).astype(BF16)
        for g in range(KVH):
            q_sc[g] = _stack_heads(q, g)
        _flash_reset(m_sc, l_sc, acc_sc)

    kv = jnp.concatenate([p[...] for p in pages], axis=0).astype(BF16)
    jb = lax.broadcasted_iota(jnp.int32, (sel_pad, keys), 0)
    kb = pg * (keys // L_SEL) + lax.broadcasted_iota(jnp.int32, (sel_pad, keys), 1) // L_SEL
    expand = jnp.where(jb == kb, 1.0, 0.0).astype(BF16)
    for g in range(KVH):
        sel_g = sel_ref[g * t:(g + 1) * t, :].astype(BF16)
        sel_rows = jnp.concatenate([sel_g] * HPG, axis=0)
        chosen = jnp.dot(sel_rows, expand, preferred_element_type=F32) > 0.5
        s = lax.dot_general(q_sc[g], kv[:, g * HD:(g + 1) * HD], _NT, preferred_element_type=F32)
        s = jnp.where(chosen, s, NEG)
        _flash_step(g, s, kv[:, gw + g * HD:gw + (g + 1) * HD], m_sc, l_sc, acc_sc)

    @pl.when(pg == n_steps - 1)
    def _():
        sg = _sigmoid(bg_ref[...])
        ni = lax.broadcasted_iota(jnp.int32, (rows, t), 1)
        nt = lax.broadcasted_iota(jnp.int32, (rows, t), 0) % t
        outs = []
        for g in range(KVH):
            kn = kvn_ref[:, g * HD:(g + 1) * HD].astype(BF16)
            vn = kvn_ref[:, gw + g * HD:gw + (g + 1) * HD].astype(BF16)
            sel_g = sel_ref[g * t:(g + 1) * t, n_past_blk:n_past_blk + 1]
            chosen = jnp.concatenate([sel_g] * HPG, axis=0) > 0.5
            s = lax.dot_general(q_sc[g], kn, _NT, preferred_element_type=F32)
            s = jnp.where(ni <= nt, jnp.where(chosen, s, NEG), NEG)
            _flash_step(g, s, vn, m_sc, l_sc, acc_sc)
            o_sel = acc_sc[g] / l_sc[g][:, 0:HD]
            for j in range(HPG):
                c = g * LANE + 3 * j + 1
                outs.append(sg[:, c:c + 1] * o_sel[j * t:(j + 1) * t])
        o_ref[...] = (ocw_ref[...] + jnp.concatenate(outs, axis=1)).astype(o_ref.dtype)


def _nsa_sample(h3, cache_cmp, cache_sel, page_table, layer, win_prev, pe_t, bd):
    b, t, _ = h3.shape
    n_pages = page_table.shape[1]
    page = cache_sel.shape[2]
    past = n_pages * page
    assert t < L_CMP and t <= L_SEL and past % L_SEL == 0 and t % 8 == 0
    nbp = past // L_CMP
    n_past_blk = past // L_SEL
    nsb = -(-(past + t) // L_SEL)
    assert nsb == n_past_blk + 1
    sel_pad = -(-nsb // LANE) * LANE
    gw = KVH * HD
    cmp = _even_odd(_cmp_past(cache_cmp, page_table, layer, pe_t, bd), 2)
    wb = win_prev.shape[1]
    wp = win_prev.reshape(b, wb, 2 * gw)
    ocw, sel = pl.pallas_call(
        functools.partial(_sample_cmp_win_kernel, t=t, past=past, nbp=nbp, nsb=nsb, sel_pad=sel_pad),
        grid=(b,),
        in_specs=[pl.BlockSpec((None, t, H_B * HD), lambda i: (i, 0, OFF_Q // (H_B * HD))),
                  pl.BlockSpec((None, t, KVH * LANE), lambda i: (i, 0, OFF_BG // (KVH * LANE))),
                  pl.BlockSpec((None, None, nbp, gw), lambda i: (i, 0, 0, 0)),
                  pl.BlockSpec((None, None, nbp, gw), lambda i: (i, 1, 0, 0)),
                  pl.BlockSpec((None, wb, 2 * gw), lambda i: (i, 0, 0)),
                  pl.BlockSpec((None, t, 2 * gw), lambda i: (i, 0, (OFF_KV + 4 * gw) // (2 * gw)))],
        out_specs=[pl.BlockSpec((None, t, H_B * HD), lambda i: (i, 0, 0)),
                   pl.BlockSpec((None, KVH * t, sel_pad), lambda i: (i, 0, 0))],
        out_shape=[jax.ShapeDtypeStruct((b, t, H_B * HD), F32),
                   jax.ShapeDtypeStruct((b, KVH * t, sel_pad), F32)],
        compiler_params=_cp(("parallel",)), name="nsa_sample_cmp_win",
    )(h3, h3, cmp, cmp, wp, h3)

    n_steps = n_pages // PAGES_PER_STEP
    specs = []
    for slot in range(PAGES_PER_STEP):
        def index_map(i, pg, pt, slot=slot):
            return (layer, pt[i, pg * PAGES_PER_STEP + slot], 0, 0)
        specs.append(pl.BlockSpec((None, None, page, 2 * gw), index_map))
    grid_spec = pltpu.PrefetchScalarGridSpec(
        num_scalar_prefetch=1, grid=(b, n_steps),
        in_specs=specs + [
            pl.BlockSpec((None, t, H_B * HD), lambda i, pg, pt: (i, 0, OFF_Q // (H_B * HD))),
            pl.BlockSpec((None, t, KVH * LANE), lambda i, pg, pt: (i, 0, OFF_BG // (KVH * LANE))),
            pl.BlockSpec((None, KVH * t, sel_pad), lambda i, pg, pt: (i, 0, 0)),
            pl.BlockSpec((None, t, 2 * gw), lambda i, pg, pt: (i, 0, (OFF_KV + 2 * gw) // (2 * gw))),
            pl.BlockSpec((None, t, H_B * HD), lambda i, pg, pt: (i, 0, 0))],
        out_specs=pl.BlockSpec((None, t, H_B * HD), lambda i, pg, pt: (i, 0, 0)),
        scratch_shapes=[pltpu.VMEM((KVH, HPG * t, HD), BF16), pltpu.VMEM((KVH, HPG * t, LANE), F32),
                        pltpu.VMEM((KVH, HPG * t, LANE), F32), pltpu.VMEM((KVH, HPG * t, HD), F32)])
    return pl.pallas_call(
        functools.partial(_sample_sel_kernel, t=t, page=page, n_steps=n_steps, n_past_blk=n_past_blk,
                          sel_pad=sel_pad),
        grid_spec=grid_spec, out_shape=jax.ShapeDtypeStruct((b, t, H_B * HD), BF16),
        compiler_params=_cp(("parallel", "arbitrary")), name="nsa_sample_sel",
    )(page_table, *([cache_sel] * PAGES_PER_STEP), h3, h3, sel, h3, ocw)


def _hgrn_lmats(c):
    nlev = int(math.log2(c))
    assert 1 << nlev == c
    r = np.arange(c)[:, None]
    i = np.arange(c)[None, :]
    mats = [i <= r, i > r]
    for lev in range(nlev):
        blk = c >> lev
        mid = (r // blk) * blk + blk // 2
        mats.append(np.where(r >= mid, (i >= mid) & (i <= r), (i > r) & (i < mid)))
    return jnp.asarray(np.concatenate(mats, axis=0).astype(np.float32), dtype=BF16)


def _hgrn_kernel(cq_ref, cf_ref, ci_ref, cg_ref, lbl_ref, ng_ref, s0_ref, lm_ref, z_ref, sn_ref, st_sc,
                 *, c, nch, layer, nlev):
    tstep = pl.program_id(2)

    @pl.when(tstep == 0)
    def _():
        st_sc[...] = jnp.transpose(s0_ref[...])

    lg = lbl_ref[...]
    e = jnp.exp(lg - jnp.max(lg, axis=0, keepdims=True))
    p = e / jnp.sum(e, axis=0, keepdims=True)
    cs = p[0:1]
    for i in range(1, layer + 1):
        cs = cs + p[i:i + 1]
    lb = cs - p[layer:layer + 1]
    log_lb = jnp.log(jnp.maximum(lb, LB_TINY))
    log_1m = jnp.log1p(-lb)
    lb_pos = lb > 0
    lm = lm_ref[...]
    cp = max(c, LANE)
    ridx = lax.broadcasted_iota(jnp.int32, (c, DK_C), 0)
    ti = lax.broadcasted_iota(jnp.int32, (c, c), 0)
    si = lax.broadcasted_iota(jnp.int32, (c, c), 1)

    def chunk(ci, carry):
        rs = pl.ds(pl.multiple_of(ci * c, c), c)
        fp = cf_ref[rs, :]
        cq = cq_ref[rs, :]
        v = ci_ref[rs, :]
        cg = cg_ref[rs, :]
        log_sig = jnp.minimum(fp, 0.0) - jnp.log1p(jnp.exp(-jnp.abs(fp)))
        x2 = log_1m + log_sig
        logf = jnp.where(lb_pos, jnp.maximum(log_lb, x2) + jnp.log1p(jnp.exp(-jnp.abs(log_lb - x2))), log_sig)
        kk = (1.0 - lb) * _sigmoid(-fp)
        qq = cq * _sigmoid(cq)
        hi = logf.astype(BF16)
        lo = (logf - hi.astype(F32)).astype(BF16)
        d2 = jnp.dot(lm, jnp.concatenate([hi, lo], axis=1), preferred_element_type=F32)
        dd = d2[:, :DK_C] + d2[:, DK_C:]
        bcum = dd[0:c]
        suf = dd[c:2 * c]
        st = st_sc[...]
        o = lax.dot_general((qq * jnp.exp(bcum)).astype(BF16), st.astype(BF16), _NT, preferred_element_type=F32)
        a = jnp.zeros((c, c), F32)
        for lev in range(nlev):
            blk = c >> lev
            ee = jnp.exp(dd[(2 + lev) * c:(3 + lev) * c])
            upper = (ridx & (blk - 1)) >= blk // 2
            qt = jnp.where(upper, qq * ee, 0.0).astype(BF16)
            kt = jnp.where(upper, 0.0, kk * ee).astype(BF16)
            al = lax.dot_general(qt, kt, _NT, preferred_element_type=F32)
            sh = int(math.log2(blk))
            a = a + jnp.where((ti >> sh) == (si >> sh), al, 0.0)
        vb = v.astype(BF16)
        o = o + jnp.dot(a.astype(BF16), vb, preferred_element_type=F32)
        o = o + jnp.sum(qq * kk, axis=1, keepdims=True) * v

        kd = kk * jnp.exp(suf)
        vp = v
        if cp > c:
            zpad = jnp.zeros((cp - c, DK_C), F32)
            kd = jnp.concatenate([kd, zpad], axis=0)
            vp = jnp.concatenate([v, zpad], axis=0)
        st_sc[...] = st * jnp.exp(bcum[c - 1:c, :]) + jnp.dot(
            jnp.transpose(vp).astype(BF16), kd.astype(BF16), preferred_element_type=F32)

        y = o * lax.rsqrt(jnp.mean(o * o, axis=1, keepdims=True) + EPS) * ng_ref[...]
        z_ref[rs, :] = (y * (cg * _sigmoid(cg))).astype(z_ref.dtype)
        return carry

    lax.fori_loop(0, nch, chunk, 0)

    @pl.when(tstep == pl.num_programs(2) - 1)
    def _():
        sn_ref[...] = jnp.transpose(st_sc[...])


def _hgrn(h3, s0, lb_logits, norm_g, layer):
    b, t, _ = h3.shape
    c = 128 if t % 128 == 0 else t
    tt = _tile(t, 512)
    assert tt % c == 0
    nch = tt // c
    nlev = int(math.log2(c))
    lm = _hgrn_lmats(c)
    depth = lb_logits.shape[0]
    col = lambda off: (lambda i, h, n: (i, n, off // DK_C + h))
    z, sn = pl.pallas_call(
        functools.partial(_hgrn_kernel, c=c, nch=nch, layer=layer, nlev=nlev), grid=(b, H_C, t // tt),
        in_specs=[pl.BlockSpec((None, tt, DK_C), col(OFF_CQ)),
                  pl.BlockSpec((None, tt, DK_C), col(OFF_CF)),
                  pl.BlockSpec((None, tt, DV_C), col(OFF_CI)),
                  pl.BlockSpec((None, tt, DV_C), col(OFF_CG)),
                  pl.BlockSpec((depth, DK_C), lambda i, h, n: (0, h)),
                  pl.BlockSpec((1, DV_C), lambda i, h, n: (0, h)),
                  pl.BlockSpec((None, None, DK_C, DV_C), lambda i, h, n: (i, h, 0, 0)),
                  pl.BlockSpec(lm.shape, lambda i, h, n: (0, 0))],
        out_specs=[pl.BlockSpec((None, tt, DV_C), lambda i, h, n: (i, n, h)),
                   pl.BlockSpec((None, None, DK_C, DV_C), lambda i, h, n: (i, h, 0, 0))],
        out_shape=[jax.ShapeDtypeStruct((b, t, H_C * DV_C), BF16),
                   jax.ShapeDtypeStruct((b, H_C, DK_C, DV_C), F32)],
        scratch_shapes=[pltpu.VMEM((DV_C, DK_C), F32)],
        compiler_params=_cp(("parallel", "parallel", "arbitrary")), name="hgrn2",
    )(h3, h3, h3, h3, lb_logits, norm_g.reshape(1, H_C * DV_C), s0, lm)
    return z, sn


def _merge_kernel(za_ref, zb_ref, zc_ref, ga_ref, gb_ref, gc_ref, x_ref, wa_ref, wb_ref, wc_ref, wo_ref,
                  g2_ref, x1_ref, xn_ref):
    ya = jnp.dot(za_ref[...], wa_ref[...], preferred_element_type=F32)
    yb = jnp.dot(zb_ref[...], wb_ref[...], preferred_element_type=F32)
    yc = jnp.dot(zc_ref[...], wc_ref[...], preferred_element_type=F32)
    y = _sigmoid(ga_ref[...]) * ya + _sigmoid(gb_ref[...]) * yb + _sigmoid(gc_ref[...]) * yc
    x1 = x_ref[...] + jnp.dot(y.astype(BF16), wo_ref[...], preferred_element_type=F32)
    x1_ref[...] = x1
    xn = x1 * lax.rsqrt(jnp.mean(x1 * x1, axis=-1, keepdims=True) + EPS) * g2_ref[...]
    xn_ref[...] = xn.astype(xn_ref.dtype)


def _merge(za, zb, zc, h2, x2d, wa, wb, wc, wo, g2):
    m, d = x2d.shape
    tm = _tile(m, 256)
    act = lambda: pl.BlockSpec((tm, d), lambda i: (i, 0))
    gate = lambda k: pl.BlockSpec((tm, d), lambda i: (i, OFF_MG // d + k))
    wgt = lambda: pl.BlockSpec((d, d), lambda i: (0, 0))
    return pl.pallas_call(
        _merge_kernel, grid=(m // tm,),
        in_specs=[act(), act(), act(), gate(0), gate(1), gate(2), act(), wgt(), wgt(), wgt(), wgt(),
                  pl.BlockSpec((1, d), lambda i: (0, 0))],
        out_specs=[act(), act()],
        out_shape=[jax.ShapeDtypeStruct((m, d), F32), jax.ShapeDtypeStruct((m, d), BF16)],
        compiler_params=_cp(("parallel",)), name="merge",
    )(za, zb, zc, h2, h2, h2, x2d, wa, wb, wc, wo, g2.reshape(1, d))


def _ffn_kernel(ug_ref, uv_ref, hg_ref, hv_ref, pg_ref, pv_ref, cw_ref, wd_ref, x1_ref, gn_ref,
                x2_ref, xn_ref, st_ref, fg_sc, fv_sc, *, ktaps, tt):
    t = pl.program_id(1)
    dff = ug_ref.shape[1]
    fg_sc[0:FFN_HALO, :] = jnp.where(t == 0, pg_ref[...], hg_ref[...])
    fv_sc[0:FFN_HALO, :] = jnp.where(t == 0, pv_ref[...], hv_ref[...])
    fg_sc[FFN_HALO:FFN_HALO + tt, :] = ug_ref[...]
    fv_sc[FFN_HALO:FFN_HALO + tt, :] = uv_ref[...]
    gate = jnp.zeros((tt, dff), F32)
    val = jnp.zeros((tt, dff), F32)
    for j in range(ktaps):
        off = FFN_HALO - (ktaps - 1) + j
        gate = gate + cw_ref[j:j + 1, 0:dff] * fg_sc[pl.ds(off, tt), :]
        val = val + cw_ref[j:j + 1, dff:2 * dff] * fv_sc[pl.ds(off, tt), :]
    act = (gate * _sigmoid(gate) * val).astype(BF16)
    x2 = x1_ref[...] + jnp.dot(act, wd_ref[...], preferred_element_type=F32)
    x2_ref[...] = x2
    xn = x2 * lax.rsqrt(jnp.mean(x2 * x2, axis=-1, keepdims=True) + EPS) * gn_ref[...]
    xn_ref[...] = xn.astype(xn_ref.dtype)
    st_ref[:, 0:dff] = fg_sc[tt:tt + FFN_HALO, :]
    st_ref[:, dff:2 * dff] = fv_sc[tt:tt + FFN_HALO, :]


def _ffn_tail(u3, prev, conv_w, w_down, x1_3, g_next, xn_dtype):
    b, t, two_dff = u3.shape
    dff = two_dff // 2
    d = x1_3.shape[2]
    ktaps = conv_w.shape[0]
    assert ktaps - 1 <= FFN_HALO
    tt = _tile(t, 256)
    nt = t // tt
    ratio = tt // FFN_HALO
    halo = lambda k: (lambda i, j: (i, jnp.maximum(j * ratio - 1, 0), k))
    x2, xn, st = pl.pallas_call(
        functools.partial(_ffn_kernel, ktaps=ktaps, tt=tt), grid=(b, nt),
        in_specs=[pl.BlockSpec((None, tt, dff), lambda i, j: (i, j, 0)),
                  pl.BlockSpec((None, tt, dff), lambda i, j: (i, j, 1)),
                  pl.BlockSpec((None, FFN_HALO, dff), halo(0)),
                  pl.BlockSpec((None, FFN_HALO, dff), halo(1)),
                  pl.BlockSpec((None, FFN_HALO, dff), lambda i, j: (i, 0, 0)),
                  pl.BlockSpec((None, FFN_HALO, dff), lambda i, j: (i, 0, 1)),
                  pl.BlockSpec((ktaps, two_dff), lambda i, j: (0, 0)),
                  pl.BlockSpec((dff, d), lambda i, j: (0, 0)),
                  pl.BlockSpec((None, tt, d), lambda i, j: (i, j, 0)),
                  pl.BlockSpec((1, d), lambda i, j: (0, 0))],
        out_specs=[pl.BlockSpec((None, tt, d), lambda i, j: (i, j, 0)),
                   pl.BlockSpec((None, tt, d), lambda i, j: (i, j, 0)),
                   pl.BlockSpec((None, FFN_HALO, two_dff), lambda i, j: (i, 0, 0))],
        out_shape=[jax.ShapeDtypeStruct((b, t, d), F32), jax.ShapeDtypeStruct((b, t, d), xn_dtype),
                   jax.ShapeDtypeStruct((b, FFN_HALO, two_dff), F32)],
        scratch_shapes=[pltpu.VMEM((FFN_HALO + tt, dff), F32), pltpu.VMEM((FFN_HALO + tt, dff), F32)],
        compiler_params=_cp(("parallel", "arbitrary")), name="ffn_tail",
    )(u3, u3, u3, u3, prev, prev, conv_w, w_down, x1_3, g_next.reshape(1, d))
    return x2, xn, st[:, FFN_HALO - (ktaps - 1):]


def _front_pad(x, rows):
    return jnp.pad(x, ((0, 0), (rows - x.shape[1], 0), (0, 0)))


def _layer(x3, xn2d, lw, layer, conv_a_prev, hg_prev, ffn_prev, nsa_fn, g_next, xn_dtype):
    b, t, d = x3.shape
    h2 = _matmul(xn2d, lw['w_in'], 1024, "in_proj")
    h3 = h2.reshape(b, t, N_PACK)
    gw = KVH * HD
    kv_rows = h3[:, :, OFF_KV:OFF_KV + 4 * gw].reshape(b, t, 4, KVH, HD)
    kv_win = h3[:, :, OFF_KV + 4 * gw:OFF_KV + 6 * gw].reshape(b, t, 2, KVH, HD)

    za, conv_a_new = _conformer(h3, _front_pad(conv_a_prev, CONV_HALO), lw['conv_a_w'], lw['conv_a_b'],
                                lw['ln_a_g'], lw['ln_a_b'])
    zb = nsa_fn(h3)
    zc, hg_new = _hgrn(h3, hg_prev, lw['lb_logits'], lw['hg_norm_g'], layer)
    x1, xn2 = _merge(za.reshape(b * t, d), zb.reshape(b * t, d), zc.reshape(b * t, d), h2,
                     x3.reshape(b * t, d), lw['w_a_out'], lw['w_b_out'], lw['w_c_out'], lw['w_out'],
                     lw['norm2_g'])
    dff2 = lw['w_up'].shape[1]
    u2 = _matmul(xn2, lw['w_up'], dff2 // 2, "up_proj")
    x2, xn_next, ffn_new = _ffn_tail(u2.reshape(b, t, dff2), _front_pad(ffn_prev, FFN_HALO), lw['conv_f_w'],
                                     lw['w_down'], x1.reshape(b, t, d), g_next, xn_dtype)
    return x2, xn_next, kv_rows, kv_win, conv_a_new, hg_new, ffn_new


def kernel(x_prompt, x_sample, cache_nsa_kv, page_table, state_win_kv, state_conv_a, state_hgrn,
           state_ffn_conv, norm1_g, w_in, conv_a_w, conv_a_b, ln_a_g, ln_a_b, w_a_out, cmp_pe, cmp_w,
           w_b_out, hg_lb_logits, hg_norm_g, w_c_out, w_out, norm2_g, w_up, conv_f_w, w_down, final_g):
    depth = w_in.shape[0]
    bp, tp, d = x_prompt.shape
    bs, ts, _ = x_sample.shape
    assert d == 1024 and w_in.shape[2] == sum(IN_WIDTHS)
    n_pool, page = cache_nsa_kv.shape[1], cache_nsa_kv.shape[2]
    gw = KVH * HD
    cache_sel = cache_nsa_kv[:, :, :, 2:4].reshape(depth, n_pool, page, 2 * gw)
    cache_cmp = cache_nsa_kv[:, :, :, 0:2].reshape(depth, n_pool, page // L_CMP, L_CMP, 2, gw)
    cache_cmp = jnp.transpose(cache_cmp, (0, 4, 1, 2, 3, 5)).reshape(depth, 2, n_pool, page // L_CMP, L_CMP * gw)
    ka = conv_a_w.shape[1]
    kf = conv_f_w.shape[1]
    dff2 = w_up.shape[2]

    xp, xs = x_prompt, x_sample
    xnp = _rmsnorm(xp.reshape(bp * tp, d), norm1_g[0], BF16)
    xns = _rmsnorm(xs.reshape(bs * ts, d), norm1_g[0], BF16)
    outs = [[] for _ in range(10)]
    for l in range(depth):
        lw = {'w_in': _pack_w_in(w_in[l]), 'conv_a_w': conv_a_w[l], 'conv_a_b': conv_a_b[l],
              'ln_a_g': ln_a_g[l], 'ln_a_b': ln_a_b[l], 'w_a_out': w_a_out[l].astype(BF16),
              'w_b_out': w_b_out[l].astype(BF16), 'lb_logits': hg_lb_logits, 'hg_norm_g': hg_norm_g[l],
              'w_c_out': w_c_out[l].astype(BF16), 'w_out': w_out[l].astype(BF16), 'norm2_g': norm2_g[l],
              'w_up': w_up[l].astype(BF16), 'conv_f_w': conv_f_w[l], 'w_down': w_down[l].astype(BF16)}
        last = l == depth - 1
        g_next = final_g if last else norm1_g[l + 1]
        xn_dtype = F32 if last else BF16
        bd = _blockdiag_cmp_w(cmp_w[l])
        pe_t = _tile_pe(cmp_pe[l])
        nsa_p = functools.partial(_nsa_prompt, pe_t=pe_t, bd=bd)
        nsa_s = functools.partial(_nsa_sample, cache_cmp=cache_cmp, cache_sel=cache_sel, page_table=page_table,
                                  layer=l, win_prev=state_win_kv[l], pe_t=pe_t, bd=bd)
        xp, xnp, kv_p, win_p, ca_p, hg_p, ff_p = _layer(
            xp, xnp, lw, l, jnp.zeros((bp, ka - 1, W_A), F32), jnp.zeros((bp, H_C, DK_C, DV_C), F32),
            jnp.zeros((bp, kf - 1, dff2), F32), nsa_p, g_next, xn_dtype)
        xs, xns, kv_s, win_s, ca_s, hg_s, ff_s = _layer(
            xs, xns, lw, l, state_conv_a[l], state_hgrn[l], state_ffn_conv[l], nsa_s, g_next, xn_dtype)
        xnp = xnp.reshape(bp * tp, d)
        xns = xns.reshape(bs * ts, d)
        wp = min(WINDOW, tp)
        win_s_all = jnp.concatenate([state_win_kv[l], win_s], axis=1)
        for lst, v in zip(outs, (kv_p, kv_s, win_p[:, tp - wp:], win_s_all[:, ts:], ca_p, ca_s, hg_p, hg_s,
                                 ff_p, ff_s)):
            lst.append(v)
    return (xnp.reshape(bp, tp, d), xns.reshape(bs, ts, d)) + tuple(jnp.stack(v) for v in outs)
```

```python
import functools
import math

import jax
import jax.numpy as jnp
import numpy as np
from jax import lax
from jax.experimental import pallas as pl
from jax.experimental.pallas import tpu as pltpu

F32 = jnp.float32
BF16 = jnp.bfloat16

H_B = 16
KVH = 4
HD = 64
HPG = H_B // KVH
L_CMP = 32
L_SEL = 64
N_SEL = 16
WINDOW = 512
H_C = 8
DK_C = 128
DV_C = 128
EPS = 1e-6
NEG = -1e30
BIG = 1e4
LB_TINY = 1e-30
SCALE = HD ** -0.5

LANE = 128
VMEM_LIMIT = 56 * 1024 * 1024

CONV_HALO = 32
FFN_HALO = 8
PAGES_PER_STEP = 8
NSB_PAD = 128
ROW_BLOCK = 256


def _cp(sem, vmem=VMEM_LIMIT):
    return pltpu.CompilerParams(dimension_semantics=sem, vmem_limit_bytes=vmem)


def _tile(n, pref):
    t = min(n, pref)
    while n % t:
        t -= 8
    assert t > 0
    return t


def _sigmoid(x):
    return 1.0 / (1.0 + jnp.exp(-x))


W_A = 1024
OFF_A = 0
OFF_GT = OFF_A + W_A
OFF_Q = OFF_GT + W_A
OFF_MG = OFF_Q + H_B * HD
OFF_CQ = OFF_MG + 3 * 1024
OFF_CF = OFF_CQ + H_C * DK_C
OFF_CI = OFF_CF + H_C * DK_C
OFF_CG = OFF_CI + H_C * DV_C
OFF_KV = OFF_CG + H_C * DV_C
OFF_BG = OFF_KV + 6 * KVH * HD
N_PACK = OFF_BG + KVH * LANE
IN_WIDTHS = (2 * W_A, H_B * HD, 6 * KVH * HD, 3 * H_B, H_C * DK_C, H_C * DK_C, H_C * DV_C, H_C * DV_C, 3 * 1024)


def _pack_w_in(w):
    d = w.shape[0]
    cuts = [int(c) for c in np.cumsum(IN_WIDTHS)[:-1]]
    a_in, b_q, b_kv, b_g, c_q, c_f, c_i, c_g, m_g = jnp.split(w, cuts, axis=1)
    bg = b_g.reshape(d, KVH, HPG * 3)
    bg = jnp.pad(bg, ((0, 0), (0, 0), (0, LANE - HPG * 3))).reshape(d, KVH * LANE)
    return jnp.concatenate([a_in, b_q, m_g, c_q, c_f, c_i, c_g, b_kv, bg], axis=1).astype(BF16)


def _rmsnorm_kernel(x_ref, g_ref, o_ref):
    x = x_ref[...]
    y = x * lax.rsqrt(jnp.mean(x * x, axis=-1, keepdims=True) + EPS)
    o_ref[...] = (y * g_ref[...]).astype(o_ref.dtype)


def _rmsnorm(x2d, g, out_dtype):
    m, d = x2d.shape
    tm = _tile(m, 512)
    return pl.pallas_call(
        _rmsnorm_kernel, grid=(m // tm,),
        in_specs=[pl.BlockSpec((tm, d), lambda i: (i, 0)), pl.BlockSpec((1, d), lambda i: (0, 0))],
        out_specs=pl.BlockSpec((tm, d), lambda i: (i, 0)),
        out_shape=jax.ShapeDtypeStruct((m, d), out_dtype),
        compiler_params=_cp(("parallel",)), name="rmsnorm")(x2d, g.reshape(1, d))


def _matmul_kernel(x_ref, w_ref, o_ref):
    o_ref[...] = jnp.dot(x_ref[...], w_ref[...], preferred_element_type=F32)


def _matmul(x, w, tn, name):
    m, k = x.shape
    n = w.shape[1]
    tm = _tile(m, 512)
    assert n % tn == 0
    return pl.pallas_call(
        _matmul_kernel, grid=(n // tn, m // tm),
        in_specs=[pl.BlockSpec((tm, k), lambda j, i: (i, 0)), pl.BlockSpec((k, tn), lambda j, i: (0, j))],
        out_specs=pl.BlockSpec((tm, tn), lambda j, i: (i, j)),
        out_shape=jax.ShapeDtypeStruct((m, n), F32),
        compiler_params=_cp(("parallel", "parallel")), name=name)(x, w)


def _conformer_kernel(a_ref, g_ref, ah_ref, gh_ref, prev_ref, w_ref, b_ref, lg_ref, lb_ref,
                      z_ref, st_ref, full_ref, y_ref, *, ktaps, tt):
    t = pl.program_id(1)
    u_halo = ah_ref[...] * _sigmoid(gh_ref[...])
    full_ref[0:CONV_HALO, :] = jnp.where(t == 0, prev_ref[...], u_halo)
    full_ref[CONV_HALO:CONV_HALO + tt, :] = a_ref[...] * _sigmoid(g_ref[...])
    width = y_ref.shape[1]

    def lane_chunk(c, carry):
        cs = pl.ds(pl.multiple_of(c * LANE, LANE), LANE)
        acc = jnp.broadcast_to(b_ref[:, cs], (tt, LANE))
        for j in range(ktaps):
            acc = acc + w_ref[j:j + 1, cs] * full_ref[pl.ds(CONV_HALO - (ktaps - 1) + j, tt), cs]
        y_ref[:, cs] = acc
        return carry

    lax.fori_loop(0, width // LANE, lane_chunk, 0)
    y = y_ref[...]
    xc = y - jnp.mean(y, axis=-1, keepdims=True)
    yn = xc * lax.rsqrt(jnp.mean(xc * xc, axis=-1, keepdims=True) + EPS) * lg_ref[...] + lb_ref[...]
    z_ref[...] = (yn * _sigmoid(yn)).astype(z_ref.dtype)
    st_ref[...] = full_ref[tt:tt + CONV_HALO, :]


def _conformer(h3, prev, conv_w, conv_b, ln_g, ln_b):
    b, t, _ = h3.shape
    ktaps = conv_w.shape[0]
    assert ktaps - 1 <= CONV_HALO
    tt = _tile(t, 256)
    nt = t // tt
    if t >= CONV_HALO:
        assert tt % CONV_HALO == 0
        halo_src = h3
        ratio = tt // CONV_HALO
        halo_a = lambda i, j: (i, jnp.maximum(j * ratio - 1, 0), OFF_A // W_A)
        halo_g = lambda i, j: (i, jnp.maximum(j * ratio - 1, 0), OFF_GT // W_A)
    else:
        assert nt == 1
        halo_src = jnp.zeros((b, CONV_HALO, W_A), F32)
        halo_a = lambda i, j: (i, 0, 0)
        halo_g = halo_a
    row = lambda v: v.reshape(1, W_A)
    z, st = pl.pallas_call(
        functools.partial(_conformer_kernel, ktaps=ktaps, tt=tt), grid=(b, nt),
        in_specs=[pl.BlockSpec((None, tt, W_A), lambda i, j: (i, j, OFF_A // W_A)),
                  pl.BlockSpec((None, tt, W_A), lambda i, j: (i, j, OFF_GT // W_A)),
                  pl.BlockSpec((None, CONV_HALO, W_A), halo_a),
                  pl.BlockSpec((None, CONV_HALO, W_A), halo_g),
                  pl.BlockSpec((None, CONV_HALO, W_A), lambda i, j: (i, 0, 0)),
                  pl.BlockSpec((ktaps, W_A), lambda i, j: (0, 0)),
                  pl.BlockSpec((1, W_A), lambda i, j: (0, 0)),
                  pl.BlockSpec((1, W_A), lambda i, j: (0, 0)),
                  pl.BlockSpec((1, W_A), lambda i, j: (0, 0))],
        out_specs=[pl.BlockSpec((None, tt, W_A), lambda i, j: (i, j, 0)),
                   pl.BlockSpec((None, CONV_HALO, W_A), lambda i, j: (i, 0, 0))],
        out_shape=[jax.ShapeDtypeStruct((b, t, W_A), BF16), jax.ShapeDtypeStruct((b, CONV_HALO, W_A), F32)],
        scratch_shapes=[pltpu.VMEM((CONV_HALO + tt, W_A), F32), pltpu.VMEM((tt, W_A), F32)],
        compiler_params=_cp(("parallel", "arbitrary")), name="conformer",
    )(h3, h3, halo_src, halo_src, prev, conv_w, row(conv_b), row(ln_g), row(ln_b))
    return z, st[:, CONV_HALO - (ktaps - 1):]


def _blockdiag_cmp_w(cw):
    eye = jnp.eye(KVH, dtype=cw.dtype)
    bd = jnp.einsum('klde,gh->klgdhe', cw, eye)
    return bd.reshape(2, L_CMP * KVH * HD, KVH * HD).astype(BF16)


def _tile_pe(pe):
    return jnp.broadcast_to(pe[:, :, None, :], (2, L_CMP, KVH, HD)).reshape(2, 1, L_CMP * KVH * HD)


def _compress_kernel(x_ref, pe_ref, w_ref, o_ref):
    x = (x_ref[...] + pe_ref[...]).astype(BF16)
    o_ref[...] = jnp.dot(x, w_ref[...], preferred_element_type=F32)


def _compress(x, pe_t, bd):
    _, r, kdim = x.shape
    n = KVH * HD
    tr = _tile(r, 256)
    return pl.pallas_call(
        _compress_kernel, grid=(2, r // tr),
        in_specs=[pl.BlockSpec((None, tr, kdim), lambda s, i: (s, i, 0)),
                  pl.BlockSpec((None, 1, kdim), lambda s, i: (s, 0, 0)),
                  pl.BlockSpec((None, kdim, n), lambda s, i: (s, 0, 0))],
        out_specs=pl.BlockSpec((None, tr, n), lambda s, i: (s, i, 0)),
        out_shape=jax.ShapeDtypeStruct((2, r, n), F32),
        compiler_params=_cp(("parallel", "parallel")), name="nsa_compress")(x, pe_t, bd)


def _even_odd(x, axis):
    ev = lax.slice_in_dim(x, 0, x.shape[axis], 2, axis)
    od = lax.slice_in_dim(x, 1, x.shape[axis], 2, axis)
    return jnp.concatenate([ev, od], axis=axis)


def _rank_select_rows(score, k):
    n = score.shape[0]
    jrow = lax.broadcasted_iota(jnp.int32, score.shape, 0)
    cnt = jnp.zeros(score.shape, F32)
    for i in range(n):
        si = score[i:i + 1, :]
        ge = jnp.where(si >= score, 1.0, 0.0)
        gt = jnp.where(si > score, 1.0, 0.0)
        cnt = cnt + jnp.where(jrow > i, ge, gt)
    return jnp.where(cnt < k, 1.0, 0.0)


def _topk_lanes(score, k):
    n = score.shape[-1]
    cidx = lax.broadcasted_iota(jnp.int32, score.shape, score.ndim - 1).astype(F32)
    sel = jnp.zeros(score.shape, F32)
    s = score
    for _ in range(k):
        m = jnp.max(s, axis=-1, keepdims=True)
        imin = jnp.min(jnp.where(s == m, cidx, float(n)), axis=-1, keepdims=True)
        hit = cidx == imin
        sel = jnp.where(hit, 1.0, sel)
        s = jnp.where(hit, -jnp.inf, s)
    return sel


_NT = (((1,), (1,)), ((), ()))
_TN = (((0,), (0,)), ((), ()))


def _ones_lanes(v):
    return jnp.concatenate([v, jnp.ones(v.shape[:-1] + (LANE - v.shape[-1],), v.dtype)], axis=-1)


def _flash_update(m_prev, acc_prev, s, v_aug, nt=False):
    m_new = jnp.maximum(m_prev, jnp.max(s, axis=1, keepdims=True))
    alpha = jnp.exp(m_prev - m_new)
    p = jnp.exp(s - m_new[:, 0:1]).astype(BF16)
    if nt:
        pv = lax.dot_general(p, v_aug, _NT, preferred_element_type=F32)
    else:
        pv = jnp.dot(p, v_aug, preferred_element_type=F32)
    return m_new, acc_prev * alpha + pv


def _flash_step(j, s, v_aug, m_sc, acc_sc, nt=False):
    m_sc[j], acc_sc[j] = _flash_update(m_sc[j], acc_sc[j], s, v_aug, nt)


def _flash_reset(m_sc, acc_sc):
    m_sc[...] = jnp.full(m_sc.shape, NEG, F32)
    acc_sc[...] = jnp.zeros(acc_sc.shape, F32)


def _flash_out(acc):
    return acc[:, 0:HD] / acc[:, HD:2 * HD]


def _nsa_prompt_kernel(q_ref, bg_ref, kc_ref, vc_ref, ks_ref, vs_ref, kw_ref, vw_ref, ex_ref, o_ref,
                       q_sc, selm_sc, m_sc, acc_sc, win_sc, bias_sc, *, tq, tk, wk, nb, nsb):
    t0 = pl.program_id(2) * tq
    q = (q_ref[...] * SCALE).astype(BF16)
    for j in range(HPG):
        q_sc[j] = q[:, j * HD:(j + 1) * HD]
    kc = kc_ref[...]
    vc = vc_ref[...]

    half = nb // 2
    row = lax.broadcasted_iota(jnp.int32, (nb, tq), 0)
    tpos = t0 + lax.broadcasted_iota(jnp.int32, (nb, tq), 1)
    blk = jnp.where(row < half, 2 * row, 2 * (row - half) + 1)
    cvalid = (blk + 1) * L_CMP - 1 <= tpos
    pg = jnp.zeros((nb, tq), F32)
    o_cmp = []
    for j in range(HPG):
        st = lax.dot_general(kc, q_sc[j], _NT, preferred_element_type=F32)
        sm = jnp.where(cvalid, st, NEG)
        e = jnp.exp(sm - jnp.max(sm, axis=0, keepdims=True))
        p = jnp.where(cvalid, e / jnp.sum(e, axis=0, keepdims=True), 0.0)
        pg = pg + p
        o_cmp.append(lax.dot_general(p.astype(BF16), vc, _TN, preferred_element_type=F32))

    ps = pg[:half] + pg[half:]
    jrow = lax.broadcasted_iota(jnp.int32, (nsb, tq), 0)
    tp = t0 + lax.broadcasted_iota(jnp.int32, (nsb, tq), 1)
    cur = tp // L_SEL
    forced_ps = jnp.where(jrow == 0, BIG, jnp.where(jrow == cur, BIG, jnp.where(jrow == cur - 1, BIG, ps)))
    score = jnp.where(jrow * L_SEL <= tp, forced_ps, -2.0 * BIG)
    sel = _rank_select_rows(score, min(N_SEL, nsb))
    sel = jnp.concatenate([sel, jnp.zeros((NSB_PAD - nsb, tq), F32)], axis=0)
    selm_sc[...] = jnp.transpose(sel).astype(BF16)

    _flash_reset(m_sc, acc_sc)
    nrb = tq // ROW_BLOCK

    def sel_chunk(c, carry):
        k0 = pl.multiple_of(c * tk, tk)
        chosen = jnp.dot(selm_sc[...], ex_ref[:, pl.ds(k0, tk)], preferred_element_type=F32)
        qpos = t0 + lax.broadcasted_iota(jnp.int32, (tq, tk), 0)
        kpos = k0 + lax.broadcasted_iota(jnp.int32, (tq, tk), 1)
        bias_sc[:, 0:tk] = jnp.where(kpos <= qpos, (chosen - 1.0) * (-NEG), NEG)

        def rows(rb, carry2):
            rs = pl.ds(pl.multiple_of(rb * ROW_BLOCK, ROW_BLOCK), ROW_BLOCK)
            ks = ks_ref[pl.ds(k0, tk), :]
            vs = vs_ref[pl.ds(k0, tk), :]
            bias = bias_sc[rs, 0:tk]
            heads = range(HPG)
            ss = [lax.dot_general(q_sc[j, rs, :], ks, _NT, preferred_element_type=F32) + bias for j in heads]
            m_prev = [m_sc[j, rs, :] for j in heads]
            m_new = [jnp.maximum(m_prev[j], jnp.max(ss[j], axis=1, keepdims=True)) for j in heads]
            ps = [jnp.exp(ss[j] - m_new[j][:, 0:1]).astype(BF16) for j in heads]
            pvs = [jnp.dot(ps[j], vs, preferred_element_type=F32) for j in heads]
            for j in heads:
                acc_sc[j, rs, :] = acc_sc[j, rs, :] * jnp.exp(m_prev[j] - m_new[j]) + pvs[j]
                m_sc[j, rs, :] = m_new[j]
            return carry2

        lax.fori_loop(0, nrb, rows, 0)
        return carry

    lax.fori_loop(0, (t0 + tq + tk - 1) // tk, sel_chunk, 0)

    w0 = pl.multiple_of(jnp.maximum(t0 - WINDOW, 0), LANE)
    d = (t0 + lax.broadcasted_iota(jnp.int32, (tq, wk), 0)) - (w0 + lax.broadcasted_iota(jnp.int32, (tq, wk), 1))
    bias_sc[:, 0:wk] = jnp.where(d >= 0, jnp.where(d < WINDOW, 0.0, NEG), NEG)

    def win_rows(rb, carry):
        rs = pl.ds(pl.multiple_of(rb * ROW_BLOCK, ROW_BLOCK), ROW_BLOCK)
        kw = kw_ref[pl.ds(w0, wk), :]
        vw = vw_ref[pl.ds(w0, wk), :]
        bias = bias_sc[rs, 0:wk]
        heads = range(HPG)
        ss = [lax.dot_general(q_sc[j, rs, :], kw, _NT, preferred_element_type=F32) + bias for j in heads]
        ps = [jnp.exp(ss[j] - jnp.max(ss[j], axis=1, keepdims=True)).astype(BF16) for j in heads]
        pvs = [jnp.dot(ps[j], vw, preferred_element_type=F32) for j in heads]
        for j in heads:
            win_sc[j, rs, :] = pvs[j]
        return carry

    lax.fori_loop(0, nrb, win_rows, 0)

    sg = _sigmoid(bg_ref[...])
    outs = []
    for j in range(HPG):
        outs.append(sg[:, 3 * j:3 * j + 1] * o_cmp[j] + sg[:, 3 * j + 1:3 * j + 2] * _flash_out(acc_sc[j])
                    + sg[:, 3 * j + 2:3 * j + 3] * _flash_out(win_sc[j]))
    o_ref[...] = jnp.concatenate(outs, axis=1).astype(o_ref.dtype)


def _nsa_prompt(h3, pe_t, bd):
    b, t, _ = h3.shape
    tq, tk = 256, 512
    wk = WINDOW + tq
    assert t % tk == 0 and t % L_SEL == 0 and t >= wk and WINDOW % LANE == 0 and tq % ROW_BLOCK == 0
    nb = t // L_CMP
    nsb = t // L_SEL
    assert nsb <= NSB_PAD and nb == 2 * nsb
    kv = h3[:, :, OFF_KV:OFF_KV + 6 * KVH * HD]
    xc = jnp.stack([kv[:, :, 0:KVH * HD].reshape(b * nb, L_CMP * KVH * HD),
                    kv[:, :, KVH * HD:2 * KVH * HD].reshape(b * nb, L_CMP * KVH * HD)])
    cmp = _compress(xc, pe_t, bd).reshape(2, b, nb, KVH, HD)
    cmp = jnp.transpose(_even_odd(cmp, 2), (0, 1, 3, 2, 4)).astype(BF16)
    kvh = jnp.transpose(kv[:, :, 2 * KVH * HD:].reshape(b, t, 4, KVH, HD), (2, 0, 3, 1, 4)).astype(BF16)
    expand = jnp.asarray(np.arange(NSB_PAD)[:, None] == np.arange(t)[None, :] // L_SEL, dtype=BF16)
    qcol = OFF_Q // (HPG * HD)
    bgcol = OFF_BG // LANE
    head = lambda w: pl.BlockSpec((None, None, t, w), lambda i, g, n: (i, g, 0, 0))
    return pl.pallas_call(
        functools.partial(_nsa_prompt_kernel, tq=tq, tk=tk, wk=wk, nb=nb, nsb=nsb), grid=(b, KVH, t // tq),
        in_specs=[pl.BlockSpec((None, tq, HPG * HD), lambda i, g, n: (i, n, qcol + g)),
                  pl.BlockSpec((None, tq, LANE), lambda i, g, n: (i, n, bgcol + g)),
                  pl.BlockSpec((None, None, nb, HD), lambda i, g, n: (i, g, 0, 0)),
                  pl.BlockSpec((None, None, nb, HD), lambda i, g, n: (i, g, 0, 0)),
                  head(HD), head(LANE), head(HD), head(LANE),
                  pl.BlockSpec((NSB_PAD, t), lambda i, g, n: (0, 0))],
        out_specs=pl.BlockSpec((None, tq, HPG * HD), lambda i, g, n: (i, n, g)),
        out_shape=jax.ShapeDtypeStruct((b, t, H_B * HD), BF16),
        scratch_shapes=[pltpu.VMEM((HPG, tq, HD), BF16), pltpu.VMEM((tq, NSB_PAD), BF16),
                        pltpu.VMEM((HPG, tq, LANE), F32), pltpu.VMEM((HPG, tq, LANE), F32),
                        pltpu.VMEM((HPG, tq, LANE), F32), pltpu.VMEM((tq, max(tk, wk)), F32)],
        compiler_params=_cp(("parallel", "parallel", "arbitrary")), name="nsa_prompt",
    )(h3, h3, cmp[0], cmp[1], kvh[0], _ones_lanes(kvh[1]), kvh[2], _ones_lanes(kvh[3]), expand)


def _cmp_perm(page):
    per = page // L_CMP
    m = np.zeros((2 * page, 2 * page), np.float32)
    for l in range(L_CMP):
        for p in range(2):
            for n in range(per):
                m[l * 2 * per + p * per + n, p * page + n * L_CMP + l] = 1.0
    return jnp.asarray(m, dtype=BF16)


def _cmp_past_kernel(pt_ref, *refs, n_steps, per):
    pages = refs[:PAGES_PER_STEP]
    pe_ref, perm_ref, w_ref, o_ref, x_sc = refs[PAGES_PER_STEP:]
    pg = pl.program_id(2)
    gd = KVH * HD
    page = pages[0].shape[-1]
    rows = PAGES_PER_STEP * per
    grp = 2 * per
    moved = []
    for pr in range(PAGES_PER_STEP // 2):
        xt = jnp.concatenate([pages[2 * pr][...].reshape(gd, page), pages[2 * pr + 1][...].reshape(gd, page)], axis=1)
        xt = (xt + pe_ref[...]).astype(BF16)
        moved.append(lax.dot_general(perm_ref[...], xt, _NT, preferred_element_type=F32))
    for l in range(L_CMP):
        slab = jnp.concatenate([m[l * grp:(l + 1) * grp] for m in moved], axis=0)
        x_sc[l, pl.ds(pl.multiple_of(pg * rows, rows), rows), :] = slab.astype(BF16)

    @pl.when(pg == n_steps - 1)
    def _():
        o_ref[...] = jnp.zeros(o_ref.shape, F32)

        def body(l, carry):
            o_ref[...] += jnp.dot(x_sc[l], w_ref[l], preferred_element_type=F32)
            return carry

        lax.fori_loop(0, L_CMP, body, 0)


def _cmp_past(cache_t, page_table, layer, pe, bd):
    b, n_pages = page_table.shape
    page = cache_t.shape[-1]
    per = page // L_CMP
    gd = KVH * HD
    assert n_pages % PAGES_PER_STEP == 0 and PAGES_PER_STEP % 2 == 0 and 2 * per == 8
    n_steps = n_pages // PAGES_PER_STEP
    nbp = n_pages * per
    pe_cols = jnp.tile(jnp.swapaxes(pe, 1, 2), (1, KVH, 2 * per))
    specs = []
    for slot in range(PAGES_PER_STEP):
        def index_map(i, s, pg, pt, slot=slot):
            return (layer, pt[i, pg * PAGES_PER_STEP + slot], s, 0, 0, 0)
        specs.append(pl.BlockSpec((None, None, None, KVH, HD, page), index_map))
    grid_spec = pltpu.PrefetchScalarGridSpec(
        num_scalar_prefetch=1, grid=(b, 2, n_steps),
        in_specs=specs + [pl.BlockSpec((None, gd, 2 * page), lambda i, s, pg, pt: (s, 0, 0)),
                          pl.BlockSpec((2 * page, 2 * page), lambda i, s, pg, pt: (0, 0)),
                          pl.BlockSpec((None, L_CMP, gd, gd), lambda i, s, pg, pt: (s, 0, 0, 0))],
        out_specs=pl.BlockSpec((None, None, nbp, gd), lambda i, s, pg, pt: (i, s, 0, 0)),
        scratch_shapes=[pltpu.VMEM((L_CMP, nbp, gd), BF16)])
    return pl.pallas_call(
        functools.partial(_cmp_past_kernel, n_steps=n_steps, per=per), grid_spec=grid_spec,
        out_shape=jax.ShapeDtypeStruct((b, 2, nbp, gd), F32),
        compiler_params=_cp(("parallel", "parallel", "arbitrary")), name="nsa_cmp_past",
    )(page_table, *([cache_t] * PAGES_PER_STEP), pe_cols, _cmp_perm(page), bd.reshape(2, L_CMP, gd, gd))


def _stack_heads(x, g):
    return jnp.concatenate([x[:, (g * HPG + j) * HD:(g * HPG + j + 1) * HD] for j in range(HPG)], axis=0)


def _sample_cmp_win_kernel(q_ref, bg_ref, kc_ref, vc_ref, wp_ref, wn_ref, ocw_ref, sel_ref,
                           *, t, past, nbp, nsb, sel_pad):
    q = (q_ref[...] * SCALE).astype(BF16)
    sg = _sigmoid(bg_ref[...])
    rows = HPG * t
    half = nbp // 2
    wb = wp_ref.shape[0]
    gw = KVH * HD

    col = lax.broadcasted_iota(jnp.int32, (rows, nbp), 1)
    tq = lax.broadcasted_iota(jnp.int32, (rows, nbp), 0) % t
    blk = jnp.where(col < half, 2 * col, 2 * (col - half) + 1)
    cvalid = (blk + 1) * L_CMP - 1 <= past + tq

    wi = lax.broadcasted_iota(jnp.int32, (rows, wb), 1)
    wt = lax.broadcasted_iota(jnp.int32, (rows, wb), 0) % t
    d_prev = wt + wb - wi
    ni = lax.broadcasted_iota(jnp.int32, (rows, t), 1)
    nt = lax.broadcasted_iota(jnp.int32, (rows, t), 0) % t
    d_new = nt - ni

    ps_all = []
    outs = [None] * H_B
    for g in range(KVH):
        qg = _stack_heads(q, g)
        kc = kc_ref[:, g * HD:(g + 1) * HD].astype(BF16)
        vc = vc_ref[:, g * HD:(g + 1) * HD].astype(BF16)
        s = lax.dot_general(qg, kc, _NT, preferred_element_type=F32)
        sm = jnp.where(cvalid, s, NEG)
        e = jnp.exp(sm - jnp.max(sm, axis=1, keepdims=True))
        p = jnp.where(cvalid, e / jnp.sum(e, axis=1, keepdims=True), 0.0)
        o_cmp = jnp.dot(p.astype(BF16), vc, preferred_element_type=F32)
        pgrp = p[0:t]
        for j in range(1, HPG):
            pgrp = pgrp + p[j * t:(j + 1) * t]
        ps_all.append(pgrp[:, :half] + pgrp[:, half:])

        kp = wp_ref[:, g * HD:(g + 1) * HD].astype(BF16)
        vp = wp_ref[:, gw + g * HD:gw + (g + 1) * HD].astype(BF16)
        kn = wn_ref[:, g * HD:(g + 1) * HD].astype(BF16)
        vn = wn_ref[:, gw + g * HD:gw + (g + 1) * HD].astype(BF16)
        s1 = lax.dot_general(qg, kp, _NT, preferred_element_type=F32)
        s1 = jnp.where(d_prev >= 0, jnp.where(d_prev < WINDOW, s1, NEG), NEG)
        s2 = lax.dot_general(qg, kn, _NT, preferred_element_type=F32)
        s2 = jnp.where(d_new >= 0, jnp.where(d_new < WINDOW, s2, NEG), NEG)
        mx = jnp.maximum(jnp.max(s1, axis=1, keepdims=True), jnp.max(s2, axis=1, keepdims=True))
        e1 = jnp.exp(s1 - mx)
        e2 = jnp.exp(s2 - mx)
        den = jnp.sum(e1, axis=1, keepdims=True) + jnp.sum(e2, axis=1, keepdims=True)
        o_win = (jnp.dot(e1.astype(BF16), vp, preferred_element_type=F32)
                 + jnp.dot(e2.astype(BF16), vn, preferred_element_type=F32)) / den
        for j in range(HPG):
            c = g * LANE + 3 * j
            outs[g * HPG + j] = (sg[:, c:c + 1] * o_cmp[j * t:(j + 1) * t]
                                 + sg[:, c + 2:c + 3] * o_win[j * t:(j + 1) * t])
    ocw_ref[...] = jnp.concatenate(outs, axis=1)

    ps = jnp.concatenate(ps_all, axis=0)
    ps = jnp.concatenate([ps, jnp.zeros((KVH * t, sel_pad - half), F32)], axis=1)
    jb = lax.broadcasted_iota(jnp.int32, (KVH * t, sel_pad), 1)
    qp = past + lax.broadcasted_iota(jnp.int32, (KVH * t, sel_pad), 0) % t
    cur = qp // L_SEL
    forced_ps = jnp.where(jb == 0, BIG, jnp.where(jb == cur, BIG, jnp.where(jb == cur - 1, BIG, ps)))
    score = jnp.where(jb * L_SEL <= qp, forced_ps, -2.0 * BIG)
    score = jnp.where(jb < nsb, score, -jnp.inf)
    sel_ref[...] = _topk_lanes(score, min(N_SEL, nsb))


def _sample_sel_kernel(pt_ref, *refs, t, n_steps):
    pages = refs[:PAGES_PER_STEP]
    q_ref, bg_ref, selp_ref, seln_ref, ex_ref, kvn_ref, ocw_ref, o_ref, q_sc, m_sc, acc_sc = refs[PAGES_PER_STEP:]
    pg = pl.program_id(1)
    rows = HPG * t
    gw = KVH * HD
    keys = PAGES_PER_STEP * pages[0].shape[-1]

    @pl.when(pg == 0)
    def _():
        q = (q_ref[...] * SCALE).astype(BF16)
        for g in range(KVH):
            q_sc[g] = _stack_heads(q, g)
        _flash_reset(m_sc, acc_sc)

    def per_head_rows(x):
        return jnp.concatenate([x[g * t:(g + 1) * t] for g in range(KVH) for _ in range(HPG)], axis=0)

    chosen = jnp.dot(per_head_rows(selp_ref[...]).astype(BF16), ex_ref[...], preferred_element_type=F32)
    ones = jnp.ones((LANE - HD, keys), BF16)
    bias = (chosen - 1.0) * (-NEG)
    groups = range(KVH)
    ss = []
    for g in groups:
        kt = jnp.concatenate([p[0, g] for p in pages], axis=1).astype(BF16)
        ss.append(jnp.dot(q_sc[g], kt, preferred_element_type=F32))
    s = jnp.concatenate(ss, axis=0) + bias
    m_prev = m_sc[...]
    m_new = jnp.maximum(m_prev, jnp.max(s, axis=1, keepdims=True))
    p = jnp.exp(s - m_new[:, 0:1]).astype(BF16)
    pvs = []
    for g in groups:
        vt = jnp.concatenate([pp[1, g] for pp in pages], axis=1).astype(BF16)
        pvs.append(lax.dot_general(p[g * rows:(g + 1) * rows], jnp.concatenate([vt, ones], axis=0), _NT,
                                   preferred_element_type=F32))
    acc_sc[...] = acc_sc[...] * jnp.exp(m_prev - m_new) + jnp.concatenate(pvs, axis=0)
    m_sc[...] = m_new

    @pl.when(pg == n_steps - 1)
    def _():
        sg = _sigmoid(bg_ref[...])
        ni = lax.broadcasted_iota(jnp.int32, (rows, t), 1)
        nt = lax.broadcasted_iota(jnp.int32, (rows, t), 0) % t
        new_chosen = per_head_rows(seln_ref[...])[:, 0:1] > 0.5
        outs = []
        for g in range(KVH):
            rs = slice(g * rows, (g + 1) * rows)
            kn = kvn_ref[:, g * HD:(g + 1) * HD].astype(BF16)
            vn = _ones_lanes(kvn_ref[:, gw + g * HD:gw + (g + 1) * HD].astype(BF16))
            s = lax.dot_general(q_sc[g], kn, _NT, preferred_element_type=F32)
            s = jnp.where(ni <= nt, jnp.where(new_chosen[rs], s, NEG), NEG)
            _flash_step(rs, s, vn, m_sc, acc_sc)
            o_sel = _flash_out(acc_sc[rs])
            for j in range(HPG):
                c = g * LANE + 3 * j + 1
                outs.append(sg[:, c:c + 1] * o_sel[j * t:(j + 1) * t])
        o_ref[...] = (ocw_ref[...] + jnp.concatenate(outs, axis=1)).astype(o_ref.dtype)


def _nsa_sample(h3, cache_t, page_table, layer, win_prev, pe, bd):
    b, t, _ = h3.shape
    n_pages = page_table.shape[1]
    page = cache_t.shape[-1]
    past = n_pages * page
    assert t < L_CMP and t <= L_SEL and past % L_SEL == 0 and t % 8 == 0
    nbp = past // L_CMP
    n_past_blk = past // L_SEL
    nsb = -(-(past + t) // L_SEL)
    assert nsb == n_past_blk + 1
    sel_pad = -(-nsb // LANE) * LANE
    gw = KVH * HD
    cmp = _even_odd(_cmp_past(cache_t, page_table, layer, pe, bd), 2)
    wb = win_prev.shape[1]
    wp = win_prev.reshape(b, wb, 2 * gw)
    ocw, sel = pl.pallas_call(
        functools.partial(_sample_cmp_win_kernel, t=t, past=past, nbp=nbp, nsb=nsb, sel_pad=sel_pad),
        grid=(b,),
        in_specs=[pl.BlockSpec((None, t, H_B * HD), lambda i: (i, 0, OFF_Q // (H_B * HD))),
                  pl.BlockSpec((None, t, KVH * LANE), lambda i: (i, 0, OFF_BG // (KVH * LANE))),
                  pl.BlockSpec((None, None, nbp, gw), lambda i: (i, 0, 0, 0)),
                  pl.BlockSpec((None, None, nbp, gw), lambda i: (i, 1, 0, 0)),
                  pl.BlockSpec((None, wb, 2 * gw), lambda i: (i, 0, 0)),
                  pl.BlockSpec((None, t, 2 * gw), lambda i: (i, 0, (OFF_KV + 4 * gw) // (2 * gw)))],
        out_specs=[pl.BlockSpec((None, t, H_B * HD), lambda i: (i, 0, 0)),
                   pl.BlockSpec((None, KVH * t, sel_pad), lambda i: (i, 0, 0))],
        out_shape=[jax.ShapeDtypeStruct((b, t, H_B * HD), F32),
                   jax.ShapeDtypeStruct((b, KVH * t, sel_pad), F32)],
        compiler_params=_cp(("parallel",)), name="nsa_sample_cmp_win",
    )(h3, h3, cmp, cmp, wp, h3)

    n_steps = n_pages // PAGES_PER_STEP
    keys = PAGES_PER_STEP * page
    bps = keys // L_SEL
    assert keys % L_SEL == 0 and bps <= LANE
    lane_pad = lambda x: jnp.pad(x, [(0, 0)] * (x.ndim - 1) + [(0, LANE - x.shape[-1])])
    sel_past = lane_pad(jnp.swapaxes(sel[:, :, :n_past_blk].reshape(b, KVH * t, n_steps, bps), 1, 2))
    sel_new = lane_pad(sel[:, :, n_past_blk:n_past_blk + 1])
    expand = jnp.asarray(np.arange(LANE)[:, None] == np.arange(keys)[None, :] // L_SEL, dtype=BF16)
    specs = []
    for slot in range(PAGES_PER_STEP):
        def index_map(i, pg, pt, slot=slot):
            return (layer, pt[i, pg * PAGES_PER_STEP + slot], 1, 0, 0, 0)
        specs.append(pl.BlockSpec((None, None, 2, KVH, HD, page), index_map))
    nrow = KVH * HPG * t
    grid_spec = pltpu.PrefetchScalarGridSpec(
        num_scalar_prefetch=1, grid=(b, n_steps),
        in_specs=specs + [
            pl.BlockSpec((None, t, H_B * HD), lambda i, pg, pt: (i, 0, OFF_Q // (H_B * HD))),
            pl.BlockSpec((None, t, KVH * LANE), lambda i, pg, pt: (i, 0, OFF_BG // (KVH * LANE))),
            pl.BlockSpec((None, None, KVH * t, LANE), lambda i, pg, pt: (i, pg, 0, 0)),
            pl.BlockSpec((None, KVH * t, LANE), lambda i, pg, pt: (i, 0, 0)),
            pl.BlockSpec((LANE, keys), lambda i, pg, pt: (0, 0)),
            pl.BlockSpec((None, t, 2 * gw), lambda i, pg, pt: (i, 0, (OFF_KV + 2 * gw) // (2 * gw))),
            pl.BlockSpec((None, t, H_B * HD), lambda i, pg, pt: (i, 0, 0))],
        out_specs=pl.BlockSpec((None, t, H_B * HD), lambda i, pg, pt: (i, 0, 0)),
        scratch_shapes=[pltpu.VMEM((KVH, HPG * t, HD), BF16), pltpu.VMEM((nrow, LANE), F32),
                        pltpu.VMEM((nrow, LANE), F32)])
    return pl.pallas_call(
        functools.partial(_sample_sel_kernel, t=t, n_steps=n_steps),
        grid_spec=grid_spec, out_shape=jax.ShapeDtypeStruct((b, t, H_B * HD), BF16),
        compiler_params=_cp(("parallel", "arbitrary")), name="nsa_sample_sel",
    )(page_table, *([cache_t] * PAGES_PER_STEP), h3, h3, sel_past, sel_new, expand, h3, ocw)


def _hgrn_lmats(c):
    nlev = int(math.log2(c))
    assert 1 << nlev == c
    r = np.arange(c)[:, None]
    i = np.arange(c)[None, :]
    mats = [i <= r, i > r]
    for lev in range(nlev):
        blk = c >> lev
        mid = (r // blk) * blk + blk // 2
        mats.append(np.where(r >= mid, (i >= mid) & (i <= r), (i > r) & (i < mid)))
    return jnp.asarray(np.concatenate(mats, axis=0).astype(np.float32), dtype=BF16)


HGRN_HEADS_PER_STEP = 2


def _hgrn_kernel(cq_ref, cf_ref, ci_ref, cg_ref, lbl_ref, ng_ref, s0_ref, lm_ref, z_ref, sn_ref, st_sc,
                 *, c, nch, layer, nlev):
    tstep = pl.program_id(2)
    nh = HGRN_HEADS_PER_STEP

    @pl.when(tstep == 0)
    def _():
        for hh in range(nh):
            st_sc[hh] = jnp.transpose(s0_ref[hh])

    lg = lbl_ref[...]
    e = jnp.exp(lg - jnp.max(lg, axis=0, keepdims=True))
    p = e / jnp.sum(e, axis=0, keepdims=True)
    cs = p[0:1]
    for i in range(1, layer + 1):
        cs = cs + p[i:i + 1]
    lb_all = cs - p[layer:layer + 1]
    lm = lm_ref[...]
    cp = max(c, LANE)
    ridx = lax.broadcasted_iota(jnp.int32, (c, DK_C), 0)
    ti = lax.broadcasted_iota(jnp.int32, (c, c), 0)
    si = lax.broadcasted_iota(jnp.int32, (c, c), 1)

    def one_head(hh, rs):
        hs = slice(hh * DK_C, (hh + 1) * DK_C)
        lb = lb_all[:, hs]
        log_lb = jnp.log(jnp.maximum(lb, LB_TINY))
        log_1m = jnp.log1p(-lb)
        fp = cf_ref[rs, hs]
        cq = cq_ref[rs, hs]
        v = ci_ref[rs, hs]
        cg = cg_ref[rs, hs]
        log_sig = jnp.minimum(fp, 0.0) - jnp.log1p(jnp.exp(-jnp.abs(fp)))
        x2 = log_1m + log_sig
        logf = jnp.where(lb > 0, jnp.maximum(log_lb, x2) + jnp.log1p(jnp.exp(-jnp.abs(log_lb - x2))), log_sig)
        kk = (1.0 - lb) * _sigmoid(-fp)
        qq = cq * _sigmoid(cq)
        hi = logf.astype(BF16)
        lo = (logf - hi.astype(F32)).astype(BF16)
        d2 = jnp.dot(lm, jnp.concatenate([hi, lo], axis=1), preferred_element_type=F32)
        dd = d2[:, :DK_C] + d2[:, DK_C:]
        bcum = dd[0:c]
        suf = dd[c:2 * c]
        st = st_sc[hh]
        o = lax.dot_general((qq * jnp.exp(bcum)).astype(BF16), st.astype(BF16), _NT, preferred_element_type=F32)
        a = jnp.zeros((c, c), F32)
        for lev in range(nlev):
            blk = c >> lev
            ee = jnp.exp(dd[(2 + lev) * c:(3 + lev) * c])
            upper = (ridx & (blk - 1)) >= blk // 2
            qt = jnp.where(upper, qq * ee, 0.0).astype(BF16)
            kt = jnp.where(upper, 0.0, kk * ee).astype(BF16)
            al = lax.dot_general(qt, kt, _NT, preferred_element_type=F32)
            sh = int(math.log2(blk))
            a = a + jnp.where((ti >> sh) == (si >> sh), al, 0.0)
        o = o + jnp.dot(a.astype(BF16), v.astype(BF16), preferred_element_type=F32)
        o = o + jnp.sum(qq * kk, axis=1, keepdims=True) * v

        kd = kk * jnp.exp(suf)
        vp = v
        if cp > c:
            zpad = jnp.zeros((cp - c, DK_C), F32)
            kd = jnp.concatenate([kd, zpad], axis=0)
            vp = jnp.concatenate([v, zpad], axis=0)
        st_sc[hh] = st * jnp.exp(bcum[c - 1:c, :]) + jnp.dot(
            jnp.transpose(vp).astype(BF16), kd.astype(BF16), preferred_element_type=F32)

        y = o * lax.rsqrt(jnp.mean(o * o, axis=1, keepdims=True) + EPS) * ng_ref[:, hs]
        z_ref[rs, hs] = (y * (cg * _sigmoid(cg))).astype(z_ref.dtype)

    def chunk(ci, carry):
        rs = pl.ds(pl.multiple_of(ci * c, c), c)
        for hh in range(nh):
            one_head(hh, rs)
        return carry

    lax.fori_loop(0, nch, chunk, 0)

    @pl.when(tstep == pl.num_programs(2) - 1)
    def _():
        for hh in range(nh):
            sn_ref[hh] = jnp.transpose(st_sc[hh])


def _hgrn(h3, s0, lb_logits, norm_g, layer):
    b, t, _ = h3.shape
    c = 128 if t % 128 == 0 else t
    tt = _tile(t, 512)
    assert tt % c == 0
    nch = tt // c
    nlev = int(math.log2(c))
    lm = _hgrn_lmats(c)
    depth = lb_logits.shape[0]
    nh = HGRN_HEADS_PER_STEP
    w = nh * DK_C
    assert H_C % nh == 0 and DK_C == DV_C
    col = lambda off: (lambda i, h, n: (i, n, off // w + h))
    z, sn = pl.pallas_call(
        functools.partial(_hgrn_kernel, c=c, nch=nch, layer=layer, nlev=nlev), grid=(b, H_C // nh, t // tt),
        in_specs=[pl.BlockSpec((None, tt, w), col(OFF_CQ)),
                  pl.BlockSpec((None, tt, w), col(OFF_CF)),
                  pl.BlockSpec((None, tt, w), col(OFF_CI)),
                  pl.BlockSpec((None, tt, w), col(OFF_CG)),
                  pl.BlockSpec((depth, w), lambda i, h, n: (0, h)),
                  pl.BlockSpec((1, w), lambda i, h, n: (0, h)),
                  pl.BlockSpec((None, nh, DK_C, DV_C), lambda i, h, n: (i, h, 0, 0)),
                  pl.BlockSpec(lm.shape, lambda i, h, n: (0, 0))],
        out_specs=[pl.BlockSpec((None, tt, w), lambda i, h, n: (i, n, h)),
                   pl.BlockSpec((None, nh, DK_C, DV_C), lambda i, h, n: (i, h, 0, 0))],
        out_shape=[jax.ShapeDtypeStruct((b, t, H_C * DV_C), BF16),
                   jax.ShapeDtypeStruct((b, H_C, DK_C, DV_C), F32)],
        scratch_shapes=[pltpu.VMEM((nh, DV_C, DK_C), F32)],
        compiler_params=_cp(("parallel", "parallel", "arbitrary")), name="hgrn2",
    )(h3, h3, h3, h3, lb_logits, norm_g.reshape(1, H_C * DV_C), s0, lm)
    return z, sn


def _merge_kernel(za_ref, zb_ref, zc_ref, ga_ref, gb_ref, gc_ref, x_ref, wa_ref, wb_ref, wc_ref, wo_ref,
                  g2_ref, x1_ref, xn_ref):
    ya = jnp.dot(za_ref[...], wa_ref[...], preferred_element_type=F32)
    yb = jnp.dot(zb_ref[...], wb_ref[...], preferred_element_type=F32)
    yc = jnp.dot(zc_ref[...], wc_ref[...], preferred_element_type=F32)
    y = _sigmoid(ga_ref[...]) * ya + _sigmoid(gb_ref[...]) * yb + _sigmoid(gc_ref[...]) * yc
    x1 = x_ref[...] + jnp.dot(y.astype(BF16), wo_ref[...], preferred_element_type=F32)
    x1_ref[...] = x1
    xn = x1 * lax.rsqrt(jnp.mean(x1 * x1, axis=-1, keepdims=True) + EPS) * g2_ref[...]
    xn_ref[...] = xn.astype(xn_ref.dtype)


def _merge(za, zb, zc, h2, x2d, wa, wb, wc, wo, g2):
    m, d = x2d.shape
    tm = _tile(m, 256)
    act = lambda: pl.BlockSpec((tm, d), lambda i: (i, 0))
    gate = lambda k: pl.BlockSpec((tm, d), lambda i: (i, OFF_MG // d + k))
    wgt = lambda: pl.BlockSpec((d, d), lambda i: (0, 0))
    return pl.pallas_call(
        _merge_kernel, grid=(m // tm,),
        in_specs=[act(), act(), act(), gate(0), gate(1), gate(2), act(), wgt(), wgt(), wgt(), wgt(),
                  pl.BlockSpec((1, d), lambda i: (0, 0))],
        out_specs=[act(), act()],
        out_shape=[jax.ShapeDtypeStruct((m, d), F32), jax.ShapeDtypeStruct((m, d), BF16)],
        compiler_params=_cp(("parallel",)), name="merge",
    )(za, zb, zc, h2, h2, h2, x2d, wa, wb, wc, wo, g2.reshape(1, d))


def _ffn_kernel(ug_ref, uv_ref, hg_ref, hv_ref, pg_ref, pv_ref, cw_ref, wd_ref, x1_ref, gn_ref,
                x2_ref, xn_ref, st_ref, fg_sc, fv_sc, *, ktaps, tt):
    t = pl.program_id(1)
    dff = ug_ref.shape[1]
    fg_sc[0:FFN_HALO, :] = jnp.where(t == 0, pg_ref[...], hg_ref[...])
    fv_sc[0:FFN_HALO, :] = jnp.where(t == 0, pv_ref[...], hv_ref[...])
    fg_sc[FFN_HALO:FFN_HALO + tt, :] = ug_ref[...]
    fv_sc[FFN_HALO:FFN_HALO + tt, :] = uv_ref[...]
    gate = jnp.zeros((tt, dff), F32)
    val = jnp.zeros((tt, dff), F32)
    for j in range(ktaps):
        off = FFN_HALO - (ktaps - 1) + j
        gate = gate + cw_ref[j:j + 1, 0:dff] * fg_sc[pl.ds(off, tt), :]
        val = val + cw_ref[j:j + 1, dff:2 * dff] * fv_sc[pl.ds(off, tt), :]
    act = (gate * _sigmoid(gate) * val).astype(BF16)
    x2 = x1_ref[...] + jnp.dot(act, wd_ref[...], preferred_element_type=F32)
    x2_ref[...] = x2
    xn = x2 * lax.rsqrt(jnp.mean(x2 * x2, axis=-1, keepdims=True) + EPS) * gn_ref[...]
    xn_ref[...] = xn.astype(xn_ref.dtype)
    st_ref[:, 0:dff] = fg_sc[tt:tt + FFN_HALO, :]
    st_ref[:, dff:2 * dff] = fv_sc[tt:tt + FFN_HALO, :]


def _ffn_tail(u3, prev, conv_w, w_down, x1_3, g_next, xn_dtype):
    b, t, two_dff = u3.shape
    dff = two_dff // 2
    d = x1_3.shape[2]
    ktaps = conv_w.shape[0]
    assert ktaps - 1 <= FFN_HALO
    tt = _tile(t, 256)
    nt = t // tt
    ratio = tt // FFN_HALO
    halo = lambda k: (lambda i, j: (i, jnp.maximum(j * ratio - 1, 0), k))
    x2, xn, st = pl.pallas_call(
        functools.partial(_ffn_kernel, ktaps=ktaps, tt=tt), grid=(b, nt),
        in_specs=[pl.BlockSpec((None, tt, dff), lambda i, j: (i, j, 0)),
                  pl.BlockSpec((None, tt, dff), lambda i, j: (i, j, 1)),
                  pl.BlockSpec((None, FFN_HALO, dff), halo(0)),
                  pl.BlockSpec((None, FFN_HALO, dff), halo(1)),
                  pl.BlockSpec((None, FFN_HALO, dff), lambda i, j: (i, 0, 0)),
                  pl.BlockSpec((None, FFN_HALO, dff), lambda i, j: (i, 0, 1)),
                  pl.BlockSpec((ktaps, two_dff), lambda i, j: (0, 0)),
                  pl.BlockSpec((dff, d), lambda i, j: (0, 0)),
                  pl.BlockSpec((None, tt, d), lambda i, j: (i, j, 0)),
                  pl.BlockSpec((1, d), lambda i, j: (0, 0))],
        out_specs=[pl.BlockSpec((None, tt, d), lambda i, j: (i, j, 0)),
                   pl.BlockSpec((None, tt, d), lambda i, j: (i, j, 0)),
                   pl.BlockSpec((None, FFN_HALO, two_dff), lambda i, j: (i, 0, 0))],
        out_shape=[jax.ShapeDtypeStruct((b, t, d), F32), jax.ShapeDtypeStruct((b, t, d), xn_dtype),
                   jax.ShapeDtypeStruct((b, FFN_HALO, two_dff), F32)],
        scratch_shapes=[pltpu.VMEM((FFN_HALO + tt, dff), F32), pltpu.VMEM((FFN_HALO + tt, dff), F32)],
        compiler_params=_cp(("parallel", "arbitrary")), name="ffn_tail",
    )(u3, u3, u3, u3, prev, prev, conv_w, w_down, x1_3, g_next.reshape(1, d))
    return x2, xn, st[:, FFN_HALO - (ktaps - 1):]


def _front_pad(x, rows):
    return jnp.pad(x, ((0, 0), (rows - x.shape[1], 0), (0, 0)))


def _layer(x3, xn2d, lw, layer, conv_a_prev, hg_prev, ffn_prev, nsa_fn, g_next, xn_dtype):
    b, t, d = x3.shape
    h2 = _matmul(xn2d, lw['w_in'], 1024, "in_proj")
    h3 = h2.reshape(b, t, N_PACK)
    gw = KVH * HD
    kv_rows = h3[:, :, OFF_KV:OFF_KV + 4 * gw].reshape(b, t, 4, KVH, HD)
    kv_win = h3[:, :, OFF_KV + 4 * gw:OFF_KV + 6 * gw].reshape(b, t, 2, KVH, HD)

    za, conv_a_new = _conformer(h3, _front_pad(conv_a_prev, CONV_HALO), lw['conv_a_w'], lw['conv_a_b'],
                                lw['ln_a_g'], lw['ln_a_b'])
    zb = nsa_fn(h3)
    zc, hg_new = _hgrn(h3, hg_prev, lw['lb_logits'], lw['hg_norm_g'], layer)
    x1, xn2 = _merge(za.reshape(b * t, d), zb.reshape(b * t, d), zc.reshape(b * t, d), h2,
                     x3.reshape(b * t, d), lw['w_a_out'], lw['w_b_out'], lw['w_c_out'], lw['w_out'],
                     lw['norm2_g'])
    dff2 = lw['w_up'].shape[1]
    u2 = _matmul(xn2, lw['w_up'], dff2 // 2, "up_proj")
    x2, xn_next, ffn_new = _ffn_tail(u2.reshape(b, t, dff2), _front_pad(ffn_prev, FFN_HALO), lw['conv_f_w'],
                                     lw['w_down'], x1.reshape(b, t, d), g_next, xn_dtype)
    return x2, xn_next, kv_rows, kv_win, conv_a_new, hg_new, ffn_new


def kernel(x_prompt, x_sample, cache_nsa_kv, page_table, state_win_kv, state_conv_a, state_hgrn,
           state_ffn_conv, norm1_g, w_in, conv_a_w, conv_a_b, ln_a_g, ln_a_b, w_a_out, cmp_pe, cmp_w,
           w_b_out, hg_lb_logits, hg_norm_g, w_c_out, w_out, norm2_g, w_up, conv_f_w, w_down, final_g):
    depth = w_in.shape[0]
    bp, tp, d = x_prompt.shape
    bs, ts, _ = x_sample.shape
    assert d == 1024 and w_in.shape[2] == sum(IN_WIDTHS)
    cache_t = jnp.transpose(cache_nsa_kv, (0, 1, 3, 4, 5, 2))
    ka = conv_a_w.shape[1]
    kf = conv_f_w.shape[1]
    dff2 = w_up.shape[2]

    xp, xs = x_prompt, x_sample
    xnp = _rmsnorm(xp.reshape(bp * tp, d), norm1_g[0], BF16)
    xns = _rmsnorm(xs.reshape(bs * ts, d), norm1_g[0], BF16)
    outs = [[] for _ in range(10)]
    for l in range(depth):
        lw = {'w_in': _pack_w_in(w_in[l]), 'conv_a_w': conv_a_w[l], 'conv_a_b': conv_a_b[l],
              'ln_a_g': ln_a_g[l], 'ln_a_b': ln_a_b[l], 'w_a_out': w_a_out[l].astype(BF16),
              'w_b_out': w_b_out[l].astype(BF16), 'lb_logits': hg_lb_logits, 'hg_norm_g': hg_norm_g[l],
              'w_c_out': w_c_out[l].astype(BF16), 'w_out': w_out[l].astype(BF16), 'norm2_g': norm2_g[l],
              'w_up': w_up[l].astype(BF16), 'conv_f_w': conv_f_w[l], 'w_down': w_down[l].astype(BF16)}
        last = l == depth - 1
        g_next = final_g if last else norm1_g[l + 1]
        xn_dtype = F32 if last else BF16
        bd = _blockdiag_cmp_w(cmp_w[l])
        pe_t = _tile_pe(cmp_pe[l])
        nsa_p = functools.partial(_nsa_prompt, pe_t=pe_t, bd=bd)
        nsa_s = functools.partial(_nsa_sample, cache_t=cache_t, page_table=page_table, layer=l,
                                  win_prev=state_win_kv[l], pe=cmp_pe[l], bd=bd)
        xp, xnp, kv_p, win_p, ca_p, hg_p, ff_p = _layer(
            xp, xnp, lw, l, jnp.zeros((bp, ka - 1, W_A), F32), jnp.zeros((bp, H_C, DK_C, DV_C), F32),
            jnp.zeros((bp, kf - 1, dff2), F32), nsa_p, g_next, xn_dtype)
        xs, xns, kv_s, win_s, ca_s, hg_s, ff_s = _layer(
            xs, xns, lw, l, state_conv_a[l], state_hgrn[l], state_ffn_conv[l], nsa_s, g_next, xn_dtype)
        xnp = xnp.reshape(bp * tp, d)
        xns = xns.reshape(bs * ts, d)
        wp = min(WINDOW, tp)
        win_s_all = jnp.concatenate([state_win_kv[l], win_s], axis=1)
        for lst, v in zip(outs, (kv_p, kv_s, win_p[:, tp - wp:], win_s_all[:, ts:], ca_p, ca_s, hg_p, hg_s,
                                 ff_p, ff_s)):
            lst.append(v)
    return (xnp.reshape(bp, tp, d), xns.reshape(bs, ts, d)) + tuple(jnp.stack(v) for v in outs)
```

```python
import functools
import math

import jax
import jax.numpy as jnp
import numpy as np
from jax import lax
from jax.experimental import pallas as pl
from jax.experimental.pallas import tpu as pltpu

F32 = jnp.float32
BF16 = jnp.bfloat16

H_B = 16
KVH = 4
HD = 64
HPG = H_B // KVH
L_CMP = 32
L_SEL = 64
N_SEL = 16
WINDOW = 512
H_C = 8
DK_C = 128
DV_C = 128
EPS = 1e-6
NEG = -1e30
BIG = 1e4
LB_TINY = 1e-30
SCALE = HD ** -0.5

LANE = 128
VMEM_LIMIT = 56 * 1024 * 1024

CONV_HALO = 32
FFN_HALO = 8
PAGES_PER_STEP = 8
NSB_PAD = 128
ROW_BLOCK = 256


def _cp(sem, vmem=VMEM_LIMIT):
    return pltpu.CompilerParams(dimension_semantics=sem, vmem_limit_bytes=vmem)


def _tile(n, pref):
    t = min(n, pref)
    while n % t:
        t -= 8
    assert t > 0
    return t


def _sigmoid(x):
    return 1.0 / (1.0 + jnp.exp(-x))


W_A = 1024
OFF_A = 0
OFF_GT = OFF_A + W_A
OFF_Q = OFF_GT + W_A
OFF_MG = OFF_Q + H_B * HD
OFF_CQ = OFF_MG + 3 * 1024
OFF_CF = OFF_CQ + H_C * DK_C
OFF_CI = OFF_CF + H_C * DK_C
OFF_CG = OFF_CI + H_C * DV_C
OFF_KV = OFF_CG + H_C * DV_C
OFF_BG = OFF_KV + 6 * KVH * HD
N_PACK = OFF_BG + KVH * LANE
IN_WIDTHS = (2 * W_A, H_B * HD, 6 * KVH * HD, 3 * H_B, H_C * DK_C, H_C * DK_C, H_C * DV_C, H_C * DV_C, 3 * 1024)


def _pack_w_in(w):
    d = w.shape[0]
    cuts = [int(c) for c in np.cumsum(IN_WIDTHS)[:-1]]
    a_in, b_q, b_kv, b_g, c_q, c_f, c_i, c_g, m_g = jnp.split(w, cuts, axis=1)
    bg = b_g.reshape(d, KVH, HPG * 3)
    bg = jnp.pad(bg, ((0, 0), (0, 0), (0, LANE - HPG * 3))).reshape(d, KVH * LANE)
    return jnp.concatenate([a_in, b_q, m_g, c_q, c_f, c_i, c_g, b_kv, bg], axis=1).astype(BF16)


def _rmsnorm_kernel(x_ref, g_ref, o_ref):
    x = x_ref[...]
    y = x * lax.rsqrt(jnp.mean(x * x, axis=-1, keepdims=True) + EPS)
    o_ref[...] = (y * g_ref[...]).astype(o_ref.dtype)


def _rmsnorm(x2d, g, out_dtype):
    m, d = x2d.shape
    tm = _tile(m, 512)
    return pl.pallas_call(
        _rmsnorm_kernel, grid=(m // tm,),
        in_specs=[pl.BlockSpec((tm, d), lambda i: (i, 0)), pl.BlockSpec((1, d), lambda i: (0, 0))],
        out_specs=pl.BlockSpec((tm, d), lambda i: (i, 0)),
        out_shape=jax.ShapeDtypeStruct((m, d), out_dtype),
        compiler_params=_cp(("parallel",)), name="rmsnorm")(x2d, g.reshape(1, d))


def _matmul_kernel(x_ref, w_ref, o_ref):
    o_ref[...] = jnp.dot(x_ref[...], w_ref[...], preferred_element_type=F32)


def _matmul(x, w, tn, name):
    m, k = x.shape
    n = w.shape[1]
    tm = _tile(m, 512)
    assert n % tn == 0
    return pl.pallas_call(
        _matmul_kernel, grid=(n // tn, m // tm),
        in_specs=[pl.BlockSpec((tm, k), lambda j, i: (i, 0)), pl.BlockSpec((k, tn), lambda j, i: (0, j))],
        out_specs=pl.BlockSpec((tm, tn), lambda j, i: (i, j)),
        out_shape=jax.ShapeDtypeStruct((m, n), F32),
        compiler_params=_cp(("parallel", "parallel")), name=name)(x, w)


def _conformer_kernel(a_ref, g_ref, ah_ref, gh_ref, prev_ref, w_ref, b_ref, lg_ref, lb_ref,
                      z_ref, st_ref, full_ref, y_ref, *, ktaps, tt):
    t = pl.program_id(1)
    u_halo = ah_ref[...] * _sigmoid(gh_ref[...])
    full_ref[0:CONV_HALO, :] = jnp.where(t == 0, prev_ref[...], u_halo)
    full_ref[CONV_HALO:CONV_HALO + tt, :] = a_ref[...] * _sigmoid(g_ref[...])
    width = y_ref.shape[1]

    def lane_chunk(c, carry):
        cs = pl.ds(pl.multiple_of(c * LANE, LANE), LANE)
        acc = jnp.broadcast_to(b_ref[:, cs], (tt, LANE))
        for j in range(ktaps):
            acc = acc + w_ref[j:j + 1, cs] * full_ref[pl.ds(CONV_HALO - (ktaps - 1) + j, tt), cs]
        y_ref[:, cs] = acc
        return carry

    lax.fori_loop(0, width // LANE, lane_chunk, 0)
    y = y_ref[...]
    xc = y - jnp.mean(y, axis=-1, keepdims=True)
    yn = xc * lax.rsqrt(jnp.mean(xc * xc, axis=-1, keepdims=True) + EPS) * lg_ref[...] + lb_ref[...]
    z_ref[...] = (yn * _sigmoid(yn)).astype(z_ref.dtype)
    st_ref[...] = full_ref[tt:tt + CONV_HALO, :]


def _conformer(h3, prev, conv_w, conv_b, ln_g, ln_b):
    b, t, _ = h3.shape
    ktaps = conv_w.shape[0]
    assert ktaps - 1 <= CONV_HALO
    tt = _tile(t, 256)
    nt = t // tt
    if t >= CONV_HALO:
        assert tt % CONV_HALO == 0
        halo_src = h3
        ratio = tt // CONV_HALO
        halo_a = lambda i, j: (i, jnp.maximum(j * ratio - 1, 0), OFF_A // W_A)
        halo_g = lambda i, j: (i, jnp.maximum(j * ratio - 1, 0), OFF_GT // W_A)
    else:
        assert nt == 1
        halo_src = jnp.zeros((b, CONV_HALO, W_A), F32)
        halo_a = lambda i, j: (i, 0, 0)
        halo_g = halo_a
    row = lambda v: v.reshape(1, W_A)
    z, st = pl.pallas_call(
        functools.partial(_conformer_kernel, ktaps=ktaps, tt=tt), grid=(b, nt),
        in_specs=[pl.BlockSpec((None, tt, W_A), lambda i, j: (i, j, OFF_A // W_A)),
                  pl.BlockSpec((None, tt, W_A), lambda i, j: (i, j, OFF_GT // W_A)),
                  pl.BlockSpec((None, CONV_HALO, W_A), halo_a),
                  pl.BlockSpec((None, CONV_HALO, W_A), halo_g),
                  pl.BlockSpec((None, CONV_HALO, W_A), lambda i, j: (i, 0, 0)),
                  pl.BlockSpec((ktaps, W_A), lambda i, j: (0, 0)),
                  pl.BlockSpec((1, W_A), lambda i, j: (0, 0)),
                  pl.BlockSpec((1, W_A), lambda i, j: (0, 0)),
                  pl.BlockSpec((1, W_A), lambda i, j: (0, 0))],
        out_specs=[pl.BlockSpec((None, tt, W_A), lambda i, j: (i, j, 0)),
                   pl.BlockSpec((None, CONV_HALO, W_A), lambda i, j: (i, 0, 0))],
        out_shape=[jax.ShapeDtypeStruct((b, t, W_A), BF16), jax.ShapeDtypeStruct((b, CONV_HALO, W_A), F32)],
        scratch_shapes=[pltpu.VMEM((CONV_HALO + tt, W_A), F32), pltpu.VMEM((tt, W_A), F32)],
        compiler_params=_cp(("parallel", "arbitrary")), name="conformer",
    )(h3, h3, halo_src, halo_src, prev, conv_w, row(conv_b), row(ln_g), row(ln_b))
    return z, st[:, CONV_HALO - (ktaps - 1):]


def _blockdiag_cmp_w(cw):
    eye = jnp.eye(KVH, dtype=cw.dtype)
    bd = jnp.einsum('klde,gh->klgdhe', cw, eye)
    return bd.reshape(2, L_CMP * KVH * HD, KVH * HD).astype(BF16)


def _tile_pe(pe):
    return jnp.broadcast_to(pe[:, :, None, :], (2, L_CMP, KVH, HD)).reshape(2, 1, L_CMP * KVH * HD)


def _compress_kernel(x_ref, pe_ref, w_ref, o_ref):
    x = (x_ref[...] + pe_ref[...]).astype(BF16)
    o_ref[...] = jnp.dot(x, w_ref[...], preferred_element_type=F32)


def _compress(x, pe_t, bd):
    _, r, kdim = x.shape
    n = KVH * HD
    tr = _tile(r, 256)
    return pl.pallas_call(
        _compress_kernel, grid=(2, r // tr),
        in_specs=[pl.BlockSpec((None, tr, kdim), lambda s, i: (s, i, 0)),
                  pl.BlockSpec((None, 1, kdim), lambda s, i: (s, 0, 0)),
                  pl.BlockSpec((None, kdim, n), lambda s, i: (s, 0, 0))],
        out_specs=pl.BlockSpec((None, tr, n), lambda s, i: (s, i, 0)),
        out_shape=jax.ShapeDtypeStruct((2, r, n), F32),
        compiler_params=_cp(("parallel", "parallel")), name="nsa_compress")(x, pe_t, bd)


def _even_odd(x, axis):
    n = x.shape[axis]
    y = x.reshape(x.shape[:axis] + (n // 2, 2) + x.shape[axis + 1:])
    return jnp.swapaxes(y, axis, axis + 1).reshape(x.shape)


def _rank_select_rows(score, k):
    n = score.shape[0]
    jrow = lax.broadcasted_iota(jnp.int32, score.shape, 0)
    cnt = jnp.zeros(score.shape, F32)
    for i in range(n):
        si = score[i:i + 1, :]
        ge = jnp.where(si >= score, 1.0, 0.0)
        gt = jnp.where(si > score, 1.0, 0.0)
        cnt = cnt + jnp.where(jrow > i, ge, gt)
    return jnp.where(cnt < k, 1.0, 0.0)


def _topk_lanes(score, k):
    n = score.shape[-1]
    cidx = lax.broadcasted_iota(jnp.int32, score.shape, score.ndim - 1).astype(F32)
    sel = jnp.zeros(score.shape, F32)
    s = score
    for _ in range(k):
        m = jnp.max(s, axis=-1, keepdims=True)
        imin = jnp.min(jnp.where(s == m, cidx, float(n)), axis=-1, keepdims=True)
        hit = cidx == imin
        sel = jnp.where(hit, 1.0, sel)
        s = jnp.where(hit, -jnp.inf, s)
    return sel


_NT = (((1,), (1,)), ((), ()))
_TN = (((0,), (0,)), ((), ()))


def _ones_lanes(v):
    return jnp.concatenate([v, jnp.ones(v.shape[:-1] + (LANE - v.shape[-1],), v.dtype)], axis=-1)


def _flash_update(m_prev, acc_prev, s, v_aug, nt=False):
    m_new = jnp.maximum(m_prev, jnp.max(s, axis=1, keepdims=True))
    alpha = jnp.exp(m_prev - m_new)
    p = jnp.exp(s - m_new[:, 0:1]).astype(BF16)
    if nt:
        pv = lax.dot_general(p, v_aug, _NT, preferred_element_type=F32)
    else:
        pv = jnp.dot(p, v_aug, preferred_element_type=F32)
    return m_new, acc_prev * alpha + pv


def _flash_step(j, s, v_aug, m_sc, acc_sc, nt=False):
    m_sc[j], acc_sc[j] = _flash_update(m_sc[j], acc_sc[j], s, v_aug, nt)


def _flash_reset(m_sc, acc_sc):
    m_sc[...] = jnp.full(m_sc.shape, NEG, F32)
    acc_sc[...] = jnp.zeros(acc_sc.shape, F32)


def _flash_out(acc):
    return acc[:, 0:HD] / acc[:, HD:2 * HD]


def _nsa_prompt_kernel(q_ref, bg_ref, kc_ref, vc_ref, ks_ref, vs_ref, kw_ref, vw_ref, ex_ref, o_ref,
                       q_sc, selm_sc, m_sc, acc_sc, win_sc, bias_sc, *, tq, tk, wk, nb, nsb):
    t0 = pl.program_id(2) * tq
    q = (q_ref[...] * SCALE).astype(BF16)
    for j in range(HPG):
        q_sc[j] = q[:, j * HD:(j + 1) * HD]
    kc = kc_ref[...]
    vc = vc_ref[...]

    half = nb // 2
    row = lax.broadcasted_iota(jnp.int32, (nb, tq), 0)
    tpos = t0 + lax.broadcasted_iota(jnp.int32, (nb, tq), 1)
    blk = jnp.where(row < half, 2 * row, 2 * (row - half) + 1)
    cvalid = (blk + 1) * L_CMP - 1 <= tpos
    sts = [lax.dot_general(kc, q_sc[j], _NT, preferred_element_type=F32) for j in range(HPG)]
    pcs = []
    for j in range(HPG):
        sm = jnp.where(cvalid, sts[j], NEG)
        e = jnp.exp(sm - jnp.max(sm, axis=0, keepdims=True))
        pcs.append(jnp.where(cvalid, e / jnp.sum(e, axis=0, keepdims=True), 0.0))
    o_cmp = [lax.dot_general(pcs[j].astype(BF16), vc, _TN, preferred_element_type=F32) for j in range(HPG)]
    pg = pcs[0]
    for j in range(1, HPG):
        pg = pg + pcs[j]

    ps = pg[:half] + pg[half:]
    jrow = lax.broadcasted_iota(jnp.int32, (nsb, tq), 0)
    tp = t0 + lax.broadcasted_iota(jnp.int32, (nsb, tq), 1)
    cur = tp // L_SEL
    forced_ps = jnp.where(jrow == 0, BIG, jnp.where(jrow == cur, BIG, jnp.where(jrow == cur - 1, BIG, ps)))
    score = jnp.where(jrow * L_SEL <= tp, forced_ps, -2.0 * BIG)
    sel = _rank_select_rows(score, min(N_SEL, nsb))
    sel = jnp.concatenate([sel, jnp.zeros((NSB_PAD - nsb, tq), F32)], axis=0)
    selm_sc[...] = jnp.transpose(sel).astype(BF16)

    _flash_reset(m_sc, acc_sc)
    nrb = tq // ROW_BLOCK

    def sel_chunk(c, carry):
        k0 = pl.multiple_of(c * tk, tk)
        chosen = jnp.dot(selm_sc[...], ex_ref[:, pl.ds(k0, tk)], preferred_element_type=F32)
        qpos = t0 + lax.broadcasted_iota(jnp.int32, (tq, tk), 0)
        kpos = k0 + lax.broadcasted_iota(jnp.int32, (tq, tk), 1)
        bias_sc[:, 0:tk] = jnp.where(kpos <= qpos, (chosen - 1.0) * (-NEG), NEG)

        def rows(rb, carry2):
            rs = pl.ds(pl.multiple_of(rb * ROW_BLOCK, ROW_BLOCK), ROW_BLOCK)
            ks = ks_ref[pl.ds(k0, tk), :]
            vs = vs_ref[pl.ds(k0, tk), :]
            bias = bias_sc[rs, 0:tk]
            heads = range(HPG)
            ss = [(lax.dot_general(q_sc[j, rs, :], ks, _NT, preferred_element_type=F32) + bias).astype(BF16)
                  for j in heads]
            m_prev = [m_sc[j, rs, :] for j in heads]
            m_new = [jnp.maximum(m_prev[j], jnp.max(ss[j], axis=1, keepdims=True).astype(F32)) for j in heads]
            ps = [jnp.exp(ss[j] - m_new[j][:, 0:1].astype(BF16)) for j in heads]
            pvs = [jnp.dot(ps[j], vs, preferred_element_type=F32) for j in heads]
            for j in heads:
                acc_sc[j, rs, :] = acc_sc[j, rs, :] * jnp.exp(m_prev[j] - m_new[j]) + pvs[j]
                m_sc[j, rs, :] = m_new[j]
            return carry2

        lax.fori_loop(0, nrb, rows, 0)
        return carry

    lax.fori_loop(0, (t0 + tq + tk - 1) // tk, sel_chunk, 0)

    w0 = pl.multiple_of(jnp.maximum(t0 - WINDOW, 0), LANE)
    d = (t0 + lax.broadcasted_iota(jnp.int32, (tq, wk), 0)) - (w0 + lax.broadcasted_iota(jnp.int32, (tq, wk), 1))
    bias_sc[:, 0:wk] = jnp.where(d >= 0, jnp.where(d < WINDOW, 0.0, NEG), NEG)

    def win_rows(rb, carry):
        rs = pl.ds(pl.multiple_of(rb * ROW_BLOCK, ROW_BLOCK), ROW_BLOCK)
        kw = kw_ref[pl.ds(w0, wk), :]
        vw = vw_ref[pl.ds(w0, wk), :]
        bias = bias_sc[rs, 0:wk]
        heads = range(HPG)
        ss = [(lax.dot_general(q_sc[j, rs, :], kw, _NT, preferred_element_type=F32) + bias).astype(BF16)
              for j in heads]
        ps = [jnp.exp(ss[j] - jnp.max(ss[j], axis=1, keepdims=True)) for j in heads]
        pvs = [jnp.dot(ps[j], vw, preferred_element_type=F32) for j in heads]
        for j in heads:
            win_sc[j, rs, :] = pvs[j]
        return carry

    lax.fori_loop(0, nrb, win_rows, 0)

    sg = _sigmoid(bg_ref[...])
    outs = []
    for j in range(HPG):
        outs.append(sg[:, 3 * j:3 * j + 1] * o_cmp[j] + sg[:, 3 * j + 1:3 * j + 2] * _flash_out(acc_sc[j])
                    + sg[:, 3 * j + 2:3 * j + 3] * _flash_out(win_sc[j]))
    o_ref[...] = jnp.concatenate(outs, axis=1).astype(o_ref.dtype)


def _nsa_prompt(h3, pe_t, bd):
    b, t, _ = h3.shape
    tq, tk = 256, 512
    wk = WINDOW + tq
    assert t % tk == 0 and t % L_SEL == 0 and t >= wk and WINDOW % LANE == 0 and tq % ROW_BLOCK == 0
    nb = t // L_CMP
    nsb = t // L_SEL
    assert nsb <= NSB_PAD and nb == 2 * nsb
    kv = h3[:, :, OFF_KV:OFF_KV + 6 * KVH * HD]
    xc = jnp.stack([kv[:, :, 0:KVH * HD].reshape(b * nb, L_CMP * KVH * HD),
                    kv[:, :, KVH * HD:2 * KVH * HD].reshape(b * nb, L_CMP * KVH * HD)])
    cmp = _compress(xc, pe_t, bd).reshape(2, b, nb, KVH, HD)
    cmp = jnp.transpose(_even_odd(cmp, 2), (0, 1, 3, 2, 4)).astype(BF16)
    kvh = jnp.transpose(kv[:, :, 2 * KVH * HD:].reshape(b, t, 4, KVH, HD), (2, 0, 3, 1, 4)).astype(BF16)
    expand = jnp.asarray(np.arange(NSB_PAD)[:, None] == np.arange(t)[None, :] // L_SEL, dtype=BF16)
    qcol = OFF_Q // (HPG * HD)
    bgcol = OFF_BG // LANE
    head = lambda w: pl.BlockSpec((None, None, t, w), lambda i, g, n: (i, g, 0, 0))
    return pl.pallas_call(
        functools.partial(_nsa_prompt_kernel, tq=tq, tk=tk, wk=wk, nb=nb, nsb=nsb), grid=(b, KVH, t // tq),
        in_specs=[pl.BlockSpec((None, tq, HPG * HD), lambda i, g, n: (i, n, qcol + g)),
                  pl.BlockSpec((None, tq, LANE), lambda i, g, n: (i, n, bgcol + g)),
                  pl.BlockSpec((None, None, nb, HD), lambda i, g, n: (i, g, 0, 0)),
                  pl.BlockSpec((None, None, nb, HD), lambda i, g, n: (i, g, 0, 0)),
                  head(HD), head(LANE), head(HD), head(LANE),
                  pl.BlockSpec((NSB_PAD, t), lambda i, g, n: (0, 0))],
        out_specs=pl.BlockSpec((None, tq, HPG * HD), lambda i, g, n: (i, n, g)),
        out_shape=jax.ShapeDtypeStruct((b, t, H_B * HD), BF16),
        scratch_shapes=[pltpu.VMEM((HPG, tq, HD), BF16), pltpu.VMEM((tq, NSB_PAD), BF16),
                        pltpu.VMEM((HPG, tq, LANE), F32), pltpu.VMEM((HPG, tq, LANE), F32),
                        pltpu.VMEM((HPG, tq, LANE), F32), pltpu.VMEM((tq, max(tk, wk)), F32)],
        compiler_params=_cp(("parallel", "parallel", "arbitrary")), name="nsa_prompt",
    )(h3, h3, cmp[0], cmp[1], kvh[0], _ones_lanes(kvh[1]), kvh[2], _ones_lanes(kvh[3]), expand)


def _cmp_perm(page):
    per = page // L_CMP
    m = np.zeros((2 * page, 2 * page), np.float32)
    for l in range(L_CMP):
        for p in range(2):
            for n in range(per):
                m[l * 2 * per + p * per + n, p * page + n * L_CMP + l] = 1.0
    return jnp.asarray(m, dtype=BF16)


def _cmp_past_kernel(pt_ref, *refs, n_steps, per):
    pages = refs[:PAGES_PER_STEP]
    pe_ref, perm_ref, w_ref, o_ref, x_sc = refs[PAGES_PER_STEP:]
    pg = pl.program_id(1)
    gd = KVH * HD
    page = pages[0].shape[-1]
    rows = PAGES_PER_STEP * per
    grp = 2 * per
    for s in range(2):
        moved = []
        for pr in range(PAGES_PER_STEP // 2):
            xt = jnp.concatenate([pages[2 * pr][s].reshape(gd, page), pages[2 * pr + 1][s].reshape(gd, page)], axis=1)
            xt = (xt + pe_ref[s]).astype(BF16)
            moved.append(lax.dot_general(perm_ref[...], xt, _NT, preferred_element_type=F32))
        for l in range(L_CMP):
            slab = jnp.concatenate([m[l * grp:(l + 1) * grp] for m in moved], axis=0)
            x_sc[s, l, pl.ds(pl.multiple_of(pg * rows, rows), rows), :] = slab.astype(BF16)

    @pl.when(pg == n_steps - 1)
    def _():
        o_ref[...] = jnp.zeros(o_ref.shape, F32)

        def body(l, carry):
            for s in range(2):
                o_ref[s] += jnp.dot(x_sc[s, l], w_ref[s, l], preferred_element_type=F32)
            return carry

        lax.fori_loop(0, L_CMP, body, 0)


def _cmp_past(cache_t, page_table, layer, pe, bd):
    b, n_pages = page_table.shape
    page = cache_t.shape[-1]
    per = page // L_CMP
    gd = KVH * HD
    assert n_pages % PAGES_PER_STEP == 0 and PAGES_PER_STEP % 2 == 0 and 2 * per == 8
    n_steps = n_pages // PAGES_PER_STEP
    nbp = n_pages * per
    pe_cols = jnp.tile(jnp.swapaxes(pe, 1, 2), (1, KVH, 2 * per))
    specs = []
    for slot in range(PAGES_PER_STEP):
        def index_map(i, pg, pt, slot=slot):
            return (layer, pt[i, pg * PAGES_PER_STEP + slot], 0, 0, 0, 0)
        specs.append(pl.BlockSpec((None, None, 2, KVH, HD, page), index_map))
    grid_spec = pltpu.PrefetchScalarGridSpec(
        num_scalar_prefetch=1, grid=(b, n_steps),
        in_specs=specs + [pl.BlockSpec((2, gd, 2 * page), lambda i, pg, pt: (0, 0, 0)),
                          pl.BlockSpec((2 * page, 2 * page), lambda i, pg, pt: (0, 0)),
                          pl.BlockSpec((2, L_CMP, gd, gd), lambda i, pg, pt: (0, 0, 0, 0))],
        out_specs=pl.BlockSpec((None, 2, nbp, gd), lambda i, pg, pt: (i, 0, 0, 0)),
        scratch_shapes=[pltpu.VMEM((2, L_CMP, nbp, gd), BF16)])
    return pl.pallas_call(
        functools.partial(_cmp_past_kernel, n_steps=n_steps, per=per), grid_spec=grid_spec,
        out_shape=jax.ShapeDtypeStruct((b, 2, nbp, gd), F32),
        compiler_params=_cp(("parallel", "arbitrary")), name="nsa_cmp_past",
    )(page_table, *([cache_t] * PAGES_PER_STEP), pe_cols, _cmp_perm(page), bd.reshape(2, L_CMP, gd, gd))


def _stack_heads(x, g):
    return jnp.concatenate([x[:, (g * HPG + j) * HD:(g * HPG + j + 1) * HD] for j in range(HPG)], axis=0)


def _sample_cmp_win_kernel(q_ref, bg_ref, kc_ref, vc_ref, wp_ref, wn_ref, ocw_ref, sel_ref,
                           *, t, past, nbp, nsb, sel_pad):
    q = (q_ref[...] * SCALE).astype(BF16)
    sg = _sigmoid(bg_ref[...])
    rows = HPG * t
    half = nbp // 2
    wb = wp_ref.shape[0]
    gw = KVH * HD

    col = lax.broadcasted_iota(jnp.int32, (rows, nbp), 1)
    tq = lax.broadcasted_iota(jnp.int32, (rows, nbp), 0) % t
    blk = jnp.where(col < half, 2 * col, 2 * (col - half) + 1)
    cvalid = (blk + 1) * L_CMP - 1 <= past + tq

    wi = lax.broadcasted_iota(jnp.int32, (rows, wb), 1)
    wt = lax.broadcasted_iota(jnp.int32, (rows, wb), 0) % t
    d_prev = wt + wb - wi
    ni = lax.broadcasted_iota(jnp.int32, (rows, t), 1)
    nt = lax.broadcasted_iota(jnp.int32, (rows, t), 0) % t
    d_new = nt - ni

    ps_all = []
    outs = [None] * H_B
    for g in range(KVH):
        qg = _stack_heads(q, g)
        kc = kc_ref[:, g * HD:(g + 1) * HD].astype(BF16)
        vc = vc_ref[:, g * HD:(g + 1) * HD].astype(BF16)
        s = lax.dot_general(qg, kc, _NT, preferred_element_type=F32)
        sm = jnp.where(cvalid, s, NEG)
        e = jnp.exp(sm - jnp.max(sm, axis=1, keepdims=True))
        p = jnp.where(cvalid, e / jnp.sum(e, axis=1, keepdims=True), 0.0)
        o_cmp = jnp.dot(p.astype(BF16), vc, preferred_element_type=F32)
        pgrp = p[0:t]
        for j in range(1, HPG):
            pgrp = pgrp + p[j * t:(j + 1) * t]
        ps_all.append(pgrp[:, :half] + pgrp[:, half:])

        kp = wp_ref[:, g * HD:(g + 1) * HD].astype(BF16)
        vp = wp_ref[:, gw + g * HD:gw + (g + 1) * HD].astype(BF16)
        kn = wn_ref[:, g * HD:(g + 1) * HD].astype(BF16)
        vn = wn_ref[:, gw + g * HD:gw + (g + 1) * HD].astype(BF16)
        s1 = lax.dot_general(qg, kp, _NT, preferred_element_type=F32)
        s1 = jnp.where(d_prev >= 0, jnp.where(d_prev < WINDOW, s1, NEG), NEG)
        s2 = lax.dot_general(qg, kn, _NT, preferred_element_type=F32)
        s2 = jnp.where(d_new >= 0, jnp.where(d_new < WINDOW, s2, NEG), NEG)
        mx = jnp.maximum(jnp.max(s1, axis=1, keepdims=True), jnp.max(s2, axis=1, keepdims=True))
        e1 = jnp.exp(s1 - mx)
        e2 = jnp.exp(s2 - mx)
        den = jnp.sum(e1, axis=1, keepdims=True) + jnp.sum(e2, axis=1, keepdims=True)
        o_win = (jnp.dot(e1.astype(BF16), vp, preferred_element_type=F32)
                 + jnp.dot(e2.astype(BF16), vn, preferred_element_type=F32)) / den
        for j in range(HPG):
            c = g * LANE + 3 * j
            outs[g * HPG + j] = (sg[:, c:c + 1] * o_cmp[j * t:(j + 1) * t]
                                 + sg[:, c + 2:c + 3] * o_win[j * t:(j + 1) * t])
    ocw_ref[...] = jnp.concatenate(outs, axis=1)

    ps = jnp.concatenate(ps_all, axis=0)
    ps = jnp.concatenate([ps, jnp.zeros((KVH * t, sel_pad - half), F32)], axis=1)
    jb = lax.broadcasted_iota(jnp.int32, (KVH * t, sel_pad), 1)
    qp = past + lax.broadcasted_iota(jnp.int32, (KVH * t, sel_pad), 0) % t
    cur = qp // L_SEL
    forced_ps = jnp.where(jb == 0, BIG, jnp.where(jb == cur, BIG, jnp.where(jb == cur - 1, BIG, ps)))
    score = jnp.where(jb * L_SEL <= qp, forced_ps, -2.0 * BIG)
    score = jnp.where(jb < nsb, score, -jnp.inf)
    sel_ref[...] = _topk_lanes(score, min(N_SEL, nsb))


def _sample_sel_kernel(pt_ref, *refs, t, n_steps):
    pages = refs[:PAGES_PER_STEP]
    q_ref, bg_ref, selp_ref, seln_ref, ex_ref, kvn_ref, ocw_ref, o_ref, q_sc, m_sc, acc_sc = refs[PAGES_PER_STEP:]
    pg = pl.program_id(1)
    rows = HPG * t
    gw = KVH * HD
    keys = PAGES_PER_STEP * pages[0].shape[-1]

    @pl.when(pg == 0)
    def _():
        q = (q_ref[...] * SCALE).astype(BF16)
        for g in range(KVH):
            q_sc[g] = _stack_heads(q, g)
        _flash_reset(m_sc, acc_sc)

    def per_head_rows(x):
        return jnp.concatenate([x[g * t:(g + 1) * t] for g in range(KVH) for _ in range(HPG)], axis=0)

    chosen = jnp.dot(per_head_rows(selp_ref[...]).astype(BF16), ex_ref[...], preferred_element_type=F32)
    ones = jnp.ones((LANE - HD, keys), BF16)
    bias = (chosen - 1.0) * (-NEG)
    groups = range(KVH)
    ss = []
    for g in groups:
        kt = jnp.concatenate([p[0, g] for p in pages], axis=1).astype(BF16)
        ss.append(jnp.dot(q_sc[g], kt, preferred_element_type=F32))
    s = jnp.concatenate(ss, axis=0) + bias
    m_prev = m_sc[...]
    m_new = jnp.maximum(m_prev, jnp.max(s, axis=1, keepdims=True))
    p = jnp.exp(s - m_new[:, 0:1]).astype(BF16)
    pvs = []
    for g in groups:
        vt = jnp.concatenate([pp[1, g] for pp in pages], axis=1).astype(BF16)
        pvs.append(lax.dot_general(p[g * rows:(g + 1) * rows], jnp.concatenate([vt, ones], axis=0), _NT,
                                   preferred_element_type=F32))
    acc_sc[...] = acc_sc[...] * jnp.exp(m_prev - m_new) + jnp.concatenate(pvs, axis=0)
    m_sc[...] = m_new

    @pl.when(pg == n_steps - 1)
    def _():
        sg = _sigmoid(bg_ref[...])
        ni = lax.broadcasted_iota(jnp.int32, (rows, t), 1)
        nt = lax.broadcasted_iota(jnp.int32, (rows, t), 0) % t
        new_chosen = per_head_rows(seln_ref[...])[:, 0:1] > 0.5
        outs = []
        for g in range(KVH):
            rs = slice(g * rows, (g + 1) * rows)
            kn = kvn_ref[:, g * HD:(g + 1) * HD].astype(BF16)
            vn = _ones_lanes(kvn_ref[:, gw + g * HD:gw + (g + 1) * HD].astype(BF16))
            s = lax.dot_general(q_sc[g], kn, _NT, preferred_element_type=F32)
            s = jnp.where(ni <= nt, jnp.where(new_chosen[rs], s, NEG), NEG)
            _flash_step(rs, s, vn, m_sc, acc_sc)
            o_sel = _flash_out(acc_sc[rs])
            for j in range(HPG):
                c = g * LANE + 3 * j + 1
                outs.append(sg[:, c:c + 1] * o_sel[j * t:(j + 1) * t])
        o_ref[...] = (ocw_ref[...] + jnp.concatenate(outs, axis=1)).astype(o_ref.dtype)


def _nsa_sample(h3, cache_t, page_table, layer, win_prev, pe, bd):
    b, t, _ = h3.shape
    n_pages = page_table.shape[1]
    page = cache_t.shape[-1]
    past = n_pages * page
    assert t < L_CMP and t <= L_SEL and past % L_SEL == 0 and t % 8 == 0
    nbp = past // L_CMP
    n_past_blk = past // L_SEL
    nsb = -(-(past + t) // L_SEL)
    assert nsb == n_past_blk + 1
    sel_pad = -(-nsb // LANE) * LANE
    gw = KVH * HD
    cmp = _even_odd(_cmp_past(cache_t, page_table, layer, pe, bd), 2)
    wb = win_prev.shape[1]
    wp = win_prev.reshape(b, wb, 2 * gw)
    ocw, sel = pl.pallas_call(
        functools.partial(_sample_cmp_win_kernel, t=t, past=past, nbp=nbp, nsb=nsb, sel_pad=sel_pad),
        grid=(b,),
        in_specs=[pl.BlockSpec((None, t, H_B * HD), lambda i: (i, 0, OFF_Q // (H_B * HD))),
                  pl.BlockSpec((None, t, KVH * LANE), lambda i: (i, 0, OFF_BG // (KVH * LANE))),
                  pl.BlockSpec((None, None, nbp, gw), lambda i: (i, 0, 0, 0)),
                  pl.BlockSpec((None, None, nbp, gw), lambda i: (i, 1, 0, 0)),
                  pl.BlockSpec((None, wb, 2 * gw), lambda i: (i, 0, 0)),
                  pl.BlockSpec((None, t, 2 * gw), lambda i: (i, 0, (OFF_KV + 4 * gw) // (2 * gw)))],
        out_specs=[pl.BlockSpec((None, t, H_B * HD), lambda i: (i, 0, 0)),
                   pl.BlockSpec((None, KVH * t, sel_pad), lambda i: (i, 0, 0))],
        out_shape=[jax.ShapeDtypeStruct((b, t, H_B * HD), F32),
                   jax.ShapeDtypeStruct((b, KVH * t, sel_pad), F32)],
        compiler_params=_cp(("parallel",)), name="nsa_sample_cmp_win",
    )(h3, h3, cmp, cmp, wp, h3)

    n_steps = n_pages // PAGES_PER_STEP
    keys = PAGES_PER_STEP * page
    bps = keys // L_SEL
    assert keys % L_SEL == 0 and bps <= LANE
    lane_pad = lambda x: jnp.pad(x, [(0, 0)] * (x.ndim - 1) + [(0, LANE - x.shape[-1])])
    sel_past = lane_pad(jnp.swapaxes(sel[:, :, :n_past_blk].reshape(b, KVH * t, n_steps, bps), 1, 2))
    sel_new = lane_pad(sel[:, :, n_past_blk:n_past_blk + 1])
    expand = jnp.asarray(np.arange(LANE)[:, None] == np.arange(keys)[None, :] // L_SEL, dtype=BF16)
    specs = []
    for slot in range(PAGES_PER_STEP):
        def index_map(i, pg, pt, slot=slot):
            return (layer, pt[i, pg * PAGES_PER_STEP + slot], 1, 0, 0, 0)
        specs.append(pl.BlockSpec((None, None, 2, KVH, HD, page), index_map))
    nrow = KVH * HPG * t
    grid_spec = pltpu.PrefetchScalarGridSpec(
        num_scalar_prefetch=1, grid=(b, n_steps),
        in_specs=specs + [
            pl.BlockSpec((None, t, H_B * HD), lambda i, pg, pt: (i, 0, OFF_Q // (H_B * HD))),
            pl.BlockSpec((None, t, KVH * LANE), lambda i, pg, pt: (i, 0, OFF_BG // (KVH * LANE))),
            pl.BlockSpec((None, None, KVH * t, LANE), lambda i, pg, pt: (i, pg, 0, 0)),
            pl.BlockSpec((None, KVH * t, LANE), lambda i, pg, pt: (i, 0, 0)),
            pl.BlockSpec((LANE, keys), lambda i, pg, pt: (0, 0)),
            pl.BlockSpec((None, t, 2 * gw), lambda i, pg, pt: (i, 0, (OFF_KV + 2 * gw) // (2 * gw))),
            pl.BlockSpec((None, t, H_B * HD), lambda i, pg, pt: (i, 0, 0))],
        out_specs=pl.BlockSpec((None, t, H_B * HD), lambda i, pg, pt: (i, 0, 0)),
        scratch_shapes=[pltpu.VMEM((KVH, HPG * t, HD), BF16), pltpu.VMEM((nrow, LANE), F32),
                        pltpu.VMEM((nrow, LANE), F32)])
    return pl.pallas_call(
        functools.partial(_sample_sel_kernel, t=t, n_steps=n_steps),
        grid_spec=grid_spec, out_shape=jax.ShapeDtypeStruct((b, t, H_B * HD), BF16),
        compiler_params=_cp(("parallel", "arbitrary")), name="nsa_sample_sel",
    )(page_table, *([cache_t] * PAGES_PER_STEP), h3, h3, sel_past, sel_new, expand, h3, ocw)


def _hgrn_lmats(c):
    nlev = int(math.log2(c))
    assert 1 << nlev == c
    r = np.arange(c)[:, None]
    i = np.arange(c)[None, :]
    mats = [i <= r, i > r]
    pairs = []
    for lev in range(nlev):
        blk = c >> lev
        mid = (r // blk) * blk + blk // 2
        mats.append(np.where(r >= mid, (i >= mid) & (i <= r), (i > r) & (i < mid)))
        pairs.append((r // blk == i // blk) & (r % blk >= blk // 2) & (i % blk < blk // 2))
    return (jnp.asarray(np.concatenate(mats, axis=0).astype(np.float32), dtype=BF16),
            jnp.asarray(np.stack(pairs).astype(np.float32)))


HGRN_HEADS_PER_STEP = 4


def _hgrn_kernel(cq_ref, cf_ref, ci_ref, cg_ref, lbl_ref, ng_ref, s0_ref, lm_ref, pm_ref, z_ref, sn_ref, st_sc,
                 *, c, nch, layer, nlev):
    tstep = pl.program_id(2)
    nh = HGRN_HEADS_PER_STEP

    @pl.when(tstep == 0)
    def _():
        for hh in range(nh):
            st_sc[hh] = jnp.transpose(s0_ref[hh])

    lg = lbl_ref[...]
    e = jnp.exp(lg - jnp.max(lg, axis=0, keepdims=True))
    p = e / jnp.sum(e, axis=0, keepdims=True)
    cs = p[0:1]
    for i in range(1, layer + 1):
        cs = cs + p[i:i + 1]
    lb_all = cs - p[layer:layer + 1]
    lm = lm_ref[...]
    cp = max(c, LANE)

    def chunk(ci, carry):
        rs = pl.ds(pl.multiple_of(ci * c, c), c)
        heads = range(nh)
        hsl = [slice(hh * DK_C, (hh + 1) * DK_C) for hh in heads]
        fp = [cf_ref[rs, hs] for hs in hsl]
        cq = [cq_ref[rs, hs] for hs in hsl]
        v = [ci_ref[rs, hs] for hs in hsl]
        kk, qq, l2 = [], [], []
        for hh in heads:
            lb = lb_all[:, hsl[hh]]
            log_lb = jnp.log(jnp.maximum(lb, LB_TINY))
            log_sig = jnp.minimum(fp[hh], 0.0) - jnp.log(1.0 + jnp.exp(-jnp.abs(fp[hh])))
            x2 = jnp.log1p(-lb) + log_sig
            logf = jnp.where(lb > 0, jnp.maximum(log_lb, x2) + jnp.log(1.0 + jnp.exp(-jnp.abs(log_lb - x2))),
                             log_sig)
            kk.append((1.0 - lb) * _sigmoid(-fp[hh]))
            qq.append(cq[hh] * _sigmoid(cq[hh]))
            hi = logf.astype(BF16)
            l2.append(jnp.concatenate([hi, (logf - hi.astype(F32)).astype(BF16)], axis=1))
        d2 = [jnp.dot(lm, l2[hh], preferred_element_type=F32) for hh in heads]
        dd = [d[:, :DK_C] + d[:, DK_C:] for d in d2]
        st = [st_sc[hh] for hh in heads]
        o1 = [lax.dot_general((qq[hh] * jnp.exp(dd[hh][0:c])).astype(BF16), st[hh].astype(BF16), _NT,
                              preferred_element_type=F32) for hh in heads]
        al = []
        for lev in range(nlev):
            for hh in heads:
                ee = jnp.exp(dd[hh][(2 + lev) * c:(3 + lev) * c])
                al.append(lax.dot_general((qq[hh] * ee).astype(BF16), (kk[hh] * ee).astype(BF16), _NT,
                                          preferred_element_type=F32))
        amat = []
        for hh in heads:
            acc = al[hh] * pm_ref[0]
            for lev in range(1, nlev):
                acc = acc + al[lev * nh + hh] * pm_ref[lev]
            amat.append(acc.astype(BF16))
        o2 = [jnp.dot(amat[hh], v[hh].astype(BF16), preferred_element_type=F32) for hh in heads]
        upd = []
        for hh in heads:
            kd = kk[hh] * jnp.exp(dd[hh][c:2 * c])
            vp = v[hh]
            if cp > c:
                zpad = jnp.zeros((cp - c, DK_C), F32)
                kd = jnp.concatenate([kd, zpad], axis=0)
                vp = jnp.concatenate([vp, zpad], axis=0)
            upd.append(jnp.dot(jnp.transpose(vp).astype(BF16), kd.astype(BF16), preferred_element_type=F32))
        for hh in heads:
            st_sc[hh] = st[hh] * jnp.exp(dd[hh][c - 1:c, :]) + upd[hh]
            o = o1[hh] + o2[hh] + jnp.sum(qq[hh] * kk[hh], axis=1, keepdims=True) * v[hh]
            y = o * lax.rsqrt(jnp.mean(o * o, axis=1, keepdims=True) + EPS) * ng_ref[:, hsl[hh]]
            cg = cg_ref[rs, hsl[hh]]
            z_ref[rs, hsl[hh]] = (y * (cg * _sigmoid(cg))).astype(z_ref.dtype)
        return carry

    lax.fori_loop(0, nch, chunk, 0)

    @pl.when(tstep == pl.num_programs(2) - 1)
    def _():
        for hh in range(nh):
            sn_ref[hh] = jnp.transpose(st_sc[hh])


def _hgrn(h3, s0, lb_logits, norm_g, layer):
    b, t, _ = h3.shape
    c = 128 if t % 128 == 0 else t
    tt = _tile(t, 512)
    assert tt % c == 0
    nch = tt // c
    nlev = int(math.log2(c))
    lm, pm = _hgrn_lmats(c)
    depth = lb_logits.shape[0]
    nh = HGRN_HEADS_PER_STEP
    w = nh * DK_C
    assert H_C % nh == 0 and DK_C == DV_C
    col = lambda off: (lambda i, h, n: (i, n, off // w + h))
    z, sn = pl.pallas_call(
        functools.partial(_hgrn_kernel, c=c, nch=nch, layer=layer, nlev=nlev), grid=(b, H_C // nh, t // tt),
        in_specs=[pl.BlockSpec((None, tt, w), col(OFF_CQ)),
                  pl.BlockSpec((None, tt, w), col(OFF_CF)),
                  pl.BlockSpec((None, tt, w), col(OFF_CI)),
                  pl.BlockSpec((None, tt, w), col(OFF_CG)),
                  pl.BlockSpec((depth, w), lambda i, h, n: (0, h)),
                  pl.BlockSpec((1, w), lambda i, h, n: (0, h)),
                  pl.BlockSpec((None, nh, DK_C, DV_C), lambda i, h, n: (i, h, 0, 0)),
                  pl.BlockSpec(lm.shape, lambda i, h, n: (0, 0)),
                  pl.BlockSpec(pm.shape, lambda i, h, n: (0, 0, 0))],
        out_specs=[pl.BlockSpec((None, tt, w), lambda i, h, n: (i, n, h)),
                   pl.BlockSpec((None, nh, DK_C, DV_C), lambda i, h, n: (i, h, 0, 0))],
        out_shape=[jax.ShapeDtypeStruct((b, t, H_C * DV_C), BF16),
                   jax.ShapeDtypeStruct((b, H_C, DK_C, DV_C), F32)],
        scratch_shapes=[pltpu.VMEM((nh, DV_C, DK_C), F32)],
        compiler_params=_cp(("parallel", "parallel", "arbitrary")), name="hgrn2",
    )(h3, h3, h3, h3, lb_logits, norm_g.reshape(1, H_C * DV_C), s0, lm, pm)
    return z, sn


def _merge_kernel(za_ref, zb_ref, zc_ref, ga_ref, gb_ref, gc_ref, x_ref, wa_ref, wb_ref, wc_ref, wo_ref,
                  g2_ref, x1_ref, xn_ref):
    ya = jnp.dot(za_ref[...], wa_ref[...], preferred_element_type=F32)
    yb = jnp.dot(zb_ref[...], wb_ref[...], preferred_element_type=F32)
    yc = jnp.dot(zc_ref[...], wc_ref[...], preferred_element_type=F32)
    y = _sigmoid(ga_ref[...]) * ya + _sigmoid(gb_ref[...]) * yb + _sigmoid(gc_ref[...]) * yc
    x1 = x_ref[...] + jnp.dot(y.astype(BF16), wo_ref[...], preferred_element_type=F32)
    x1_ref[...] = x1
    xn = x1 * lax.rsqrt(jnp.mean(x1 * x1, axis=-1, keepdims=True) + EPS) * g2_ref[...]
    xn_ref[...] = xn.astype(xn_ref.dtype)


def _merge(za, zb, zc, h2, x2d, wa, wb, wc, wo, g2):
    m, d = x2d.shape
    tm = _tile(m, 256)
    act = lambda: pl.BlockSpec((tm, d), lambda i: (i, 0))
    gate = lambda k: pl.BlockSpec((tm, d), lambda i: (i, OFF_MG // d + k))
    wgt = lambda: pl.BlockSpec((d, d), lambda i: (0, 0))
    return pl.pallas_call(
        _merge_kernel, grid=(m // tm,),
        in_specs=[act(), act(), act(), gate(0), gate(1), gate(2), act(), wgt(), wgt(), wgt(), wgt(),
                  pl.BlockSpec((1, d), lambda i: (0, 0))],
        out_specs=[act(), act()],
        out_shape=[jax.ShapeDtypeStruct((m, d), F32), jax.ShapeDtypeStruct((m, d), BF16)],
        compiler_params=_cp(("parallel",)), name="merge",
    )(za, zb, zc, h2, h2, h2, x2d, wa, wb, wc, wo, g2.reshape(1, d))


def _ffn_kernel(ug_ref, uv_ref, hg_ref, hv_ref, pg_ref, pv_ref, cw_ref, wd_ref, x1_ref, gn_ref,
                x2_ref, xn_ref, st_ref, fg_sc, fv_sc, *, ktaps, tt):
    t = pl.program_id(1)
    dff = ug_ref.shape[1]
    fg_sc[0:FFN_HALO, :] = jnp.where(t == 0, pg_ref[...], hg_ref[...])
    fv_sc[0:FFN_HALO, :] = jnp.where(t == 0, pv_ref[...], hv_ref[...])
    fg_sc[FFN_HALO:FFN_HALO + tt, :] = ug_ref[...]
    fv_sc[FFN_HALO:FFN_HALO + tt, :] = uv_ref[...]
    gate = jnp.zeros((tt, dff), F32)
    val = jnp.zeros((tt, dff), F32)
    for j in range(ktaps):
        off = FFN_HALO - (ktaps - 1) + j
        gate = gate + cw_ref[j:j + 1, 0:dff] * fg_sc[pl.ds(off, tt), :]
        val = val + cw_ref[j:j + 1, dff:2 * dff] * fv_sc[pl.ds(off, tt), :]
    act = (gate * _sigmoid(gate) * val).astype(BF16)
    x2 = x1_ref[...] + jnp.dot(act, wd_ref[...], preferred_element_type=F32)
    x2_ref[...] = x2
    xn = x2 * lax.rsqrt(jnp.mean(x2 * x2, axis=-1, keepdims=True) + EPS) * gn_ref[...]
    xn_ref[...] = xn.astype(xn_ref.dtype)
    st_ref[:, 0:dff] = fg_sc[tt:tt + FFN_HALO, :]
    st_ref[:, dff:2 * dff] = fv_sc[tt:tt + FFN_HALO, :]


def _ffn_tail(u3, prev, conv_w, w_down, x1_3, g_next, xn_dtype):
    b, t, two_dff = u3.shape
    dff = two_dff // 2
    d = x1_3.shape[2]
    ktaps = conv_w.shape[0]
    assert ktaps - 1 <= FFN_HALO
    tt = _tile(t, 256)
    nt = t // tt
    ratio = tt // FFN_HALO
    halo = lambda k: (lambda i, j: (i, jnp.maximum(j * ratio - 1, 0), k))
    x2, xn, st = pl.pallas_call(
        functools.partial(_ffn_kernel, ktaps=ktaps, tt=tt), grid=(b, nt),
        in_specs=[pl.BlockSpec((None, tt, dff), lambda i, j: (i, j, 0)),
                  pl.BlockSpec((None, tt, dff), lambda i, j: (i, j, 1)),
                  pl.BlockSpec((None, FFN_HALO, dff), halo(0)),
                  pl.BlockSpec((None, FFN_HALO, dff), halo(1)),
                  pl.BlockSpec((None, FFN_HALO, dff), lambda i, j: (i, 0, 0)),
                  pl.BlockSpec((None, FFN_HALO, dff), lambda i, j: (i, 0, 1)),
                  pl.BlockSpec((ktaps, two_dff), lambda i, j: (0, 0)),
                  pl.BlockSpec((dff, d), lambda i, j: (0, 0)),
                  pl.BlockSpec((None, tt, d), lambda i, j: (i, j, 0)),
                  pl.BlockSpec((1, d), lambda i, j: (0, 0))],
        out_specs=[pl.BlockSpec((None, tt, d), lambda i, j: (i, j, 0)),
                   pl.BlockSpec((None, tt, d), lambda i, j: (i, j, 0)),
                   pl.BlockSpec((None, FFN_HALO, two_dff), lambda i, j: (i, 0, 0))],
        out_shape=[jax.ShapeDtypeStruct((b, t, d), F32), jax.ShapeDtypeStruct((b, t, d), xn_dtype),
                   jax.ShapeDtypeStruct((b, FFN_HALO, two_dff), F32)],
        scratch_shapes=[pltpu.VMEM((FFN_HALO + tt, dff), F32), pltpu.VMEM((FFN_HALO + tt, dff), F32)],
        compiler_params=_cp(("parallel", "arbitrary")), name="ffn_tail",
    )(u3, u3, u3, u3, prev, prev, conv_w, w_down, x1_3, g_next.reshape(1, d))
    return x2, xn, st[:, FFN_HALO - (ktaps - 1):]


def _front_pad(x, rows):
    return jnp.pad(x, ((0, 0), (rows - x.shape[1], 0), (0, 0)))


def _layer(x3, xn2d, lw, layer, conv_a_prev, hg_prev, ffn_prev, nsa_fn, g_next, xn_dtype):
    b, t, d = x3.shape
    h2 = _matmul(xn2d, lw['w_in'], 1024, "in_proj")
    h3 = h2.reshape(b, t, N_PACK)
    gw = KVH * HD
    kv_rows = h3[:, :, OFF_KV:OFF_KV + 4 * gw].reshape(b, t, 4, KVH, HD)
    wrows = min(WINDOW, t)
    kv_win = h3[:, t - wrows:, OFF_KV + 4 * gw:OFF_KV + 6 * gw].reshape(b, wrows, 2, KVH, HD)

    za, conv_a_new = _conformer(h3, _front_pad(conv_a_prev, CONV_HALO), lw['conv_a_w'], lw['conv_a_b'],
                                lw['ln_a_g'], lw['ln_a_b'])
    zb = nsa_fn(h3)
    zc, hg_new = _hgrn(h3, hg_prev, lw['lb_logits'], lw['hg_norm_g'], layer)
    x1, xn2 = _merge(za.reshape(b * t, d), zb.reshape(b * t, d), zc.reshape(b * t, d), h2,
                     x3.reshape(b * t, d), lw['w_a_out'], lw['w_b_out'], lw['w_c_out'], lw['w_out'],
                     lw['norm2_g'])
    dff2 = lw['w_up'].shape[1]
    u2 = _matmul(xn2, lw['w_up'], dff2 // 2, "up_proj")
    x2, xn_next, ffn_new = _ffn_tail(u2.reshape(b, t, dff2), _front_pad(ffn_prev, FFN_HALO), lw['conv_f_w'],
                                     lw['w_down'], x1.reshape(b, t, d), g_next, xn_dtype)
    return x2, xn_next, kv_rows, kv_win, conv_a_new, hg_new, ffn_new


def kernel(x_prompt, x_sample, cache_nsa_kv, page_table, state_win_kv, state_conv_a, state_hgrn,
           state_ffn_conv, norm1_g, w_in, conv_a_w, conv_a_b, ln_a_g, ln_a_b, w_a_out, cmp_pe, cmp_w,
           w_b_out, hg_lb_logits, hg_norm_g, w_c_out, w_out, norm2_g, w_up, conv_f_w, w_down, final_g):
    depth = w_in.shape[0]
    bp, tp, d = x_prompt.shape
    bs, ts, _ = x_sample.shape
    assert d == 1024 and w_in.shape[2] == sum(IN_WIDTHS)
    cache_t = jnp.transpose(cache_nsa_kv, (0, 1, 3, 4, 5, 2))
    ka = conv_a_w.shape[1]
    kf = conv_f_w.shape[1]
    dff2 = w_up.shape[2]

    xp, xs = x_prompt, x_sample
    xnp = _rmsnorm(xp.reshape(bp * tp, d), norm1_g[0], BF16)
    xns = _rmsnorm(xs.reshape(bs * ts, d), norm1_g[0], BF16)
    outs = [[] for _ in range(10)]
    for l in range(depth):
        lw = {'w_in': _pack_w_in(w_in[l]), 'conv_a_w': conv_a_w[l], 'conv_a_b': conv_a_b[l],
              'ln_a_g': ln_a_g[l], 'ln_a_b': ln_a_b[l], 'w_a_out': w_a_out[l].astype(BF16),
              'w_b_out': w_b_out[l].astype(BF16), 'lb_logits': hg_lb_logits, 'hg_norm_g': hg_norm_g[l],
              'w_c_out': w_c_out[l].astype(BF16), 'w_out': w_out[l].astype(BF16), 'norm2_g': norm2_g[l],
              'w_up': w_up[l].astype(BF16), 'conv_f_w': conv_f_w[l], 'w_down': w_down[l].astype(BF16)}
        last = l == depth - 1
        g_next = final_g if last else norm1_g[l + 1]
        xn_dtype = F32 if last else BF16
        bd = _blockdiag_cmp_w(cmp_w[l])
        pe_t = _tile_pe(cmp_pe[l])
        nsa_p = functools.partial(_nsa_prompt, pe_t=pe_t, bd=bd)
        nsa_s = functools.partial(_nsa_sample, cache_t=cache_t, page_table=page_table, layer=l,
                                  win_prev=state_win_kv[l], pe=cmp_pe[l], bd=bd)
        xp, xnp, kv_p, win_p, ca_p, hg_p, ff_p = _layer(
            xp, xnp, lw, l, jnp.zeros((bp, ka - 1, W_A), F32), jnp.zeros((bp, H_C, DK_C, DV_C), F32),
            jnp.zeros((bp, kf - 1, dff2), F32), nsa_p, g_next, xn_dtype)
        xs, xns, kv_s, win_s, ca_s, hg_s, ff_s = _layer(
            xs, xns, lw, l, state_conv_a[l], state_hgrn[l], state_ffn_conv[l], nsa_s, g_next, xn_dtype)
        xnp = xnp.reshape(bp * tp, d)
        xns = xns.reshape(bs * ts, d)
        win_s_all = jnp.concatenate([state_win_kv[l], win_s], axis=1)
        for lst, v in zip(outs, (kv_p, kv_s, win_p, win_s_all[:, ts:], ca_p, ca_s, hg_p, hg_s, ff_p, ff_s)):
            lst.append(v)
    return (xnp.reshape(bp, tp, d), xns.reshape(bs, ts, d)) + tuple(jnp.stack(v) for v in outs)
```

```python
import functools
import math

import jax
import jax.numpy as jnp
import numpy as np
from jax import lax
from jax.experimental import pallas as pl
from jax.experimental.pallas import tpu as pltpu

F32 = jnp.float32
BF16 = jnp.bfloat16

H_B = 16
KVH = 4
HD = 64
HPG = H_B // KVH
L_CMP = 32
L_SEL = 64
N_SEL = 16
WINDOW = 512
H_C = 8
DK_C = 128
DV_C = 128
EPS = 1e-6
NEG = -1e30
BIG = 1e4
LB_TINY = 1e-30
SCALE = HD ** -0.5

LANE = 128
VMEM_LIMIT = 56 * 1024 * 1024

CONV_HALO = 32
FFN_HALO = 8
PAGES_PER_STEP = 16
NSB_PAD = 128
ROW_BLOCK = 256


def _cp(sem, vmem=VMEM_LIMIT):
    return pltpu.CompilerParams(dimension_semantics=sem, vmem_limit_bytes=vmem)


def _tile(n, pref):
    t = min(n, pref)
    while n % t:
        t -= 8
    assert t > 0
    return t


def _sigmoid(x):
    return 1.0 / (1.0 + jnp.exp(-x))


W_A = 1024
OFF_A = 0
OFF_GT = OFF_A + W_A
OFF_Q = OFF_GT + W_A
OFF_MG = OFF_Q + H_B * HD
OFF_CQ = OFF_MG + 3 * 1024
OFF_CF = OFF_CQ + H_C * DK_C
OFF_CI = OFF_CF + H_C * DK_C
OFF_CG = OFF_CI + H_C * DV_C
OFF_KV = OFF_CG + H_C * DV_C
OFF_BG = OFF_KV + 6 * KVH * HD
N_PACK = OFF_BG + KVH * LANE
IN_WIDTHS = (2 * W_A, H_B * HD, 6 * KVH * HD, 3 * H_B, H_C * DK_C, H_C * DK_C, H_C * DV_C, H_C * DV_C, 3 * 1024)


def _pack_w_in(w):
    d = w.shape[0]
    cuts = [int(c) for c in np.cumsum(IN_WIDTHS)[:-1]]
    a_in, b_q, b_kv, b_g, c_q, c_f, c_i, c_g, m_g = jnp.split(w, cuts, axis=1)
    bg = b_g.reshape(d, KVH, HPG * 3)
    bg = jnp.pad(bg, ((0, 0), (0, 0), (0, LANE - HPG * 3))).reshape(d, KVH * LANE)
    return jnp.concatenate([a_in, b_q, m_g, c_q, c_f, c_i, c_g, b_kv, bg], axis=1).astype(BF16)


def _rmsnorm_kernel(x_ref, g_ref, o_ref):
    x = x_ref[...]
    y = x * lax.rsqrt(jnp.mean(x * x, axis=-1, keepdims=True) + EPS)
    o_ref[...] = (y * g_ref[...]).astype(o_ref.dtype)


def _rmsnorm(x2d, g, out_dtype):
    m, d = x2d.shape
    tm = _tile(m, 512)
    return pl.pallas_call(
        _rmsnorm_kernel, grid=(m // tm,),
        in_specs=[pl.BlockSpec((tm, d), lambda i: (i, 0)), pl.BlockSpec((1, d), lambda i: (0, 0))],
        out_specs=pl.BlockSpec((tm, d), lambda i: (i, 0)),
        out_shape=jax.ShapeDtypeStruct((m, d), out_dtype),
        compiler_params=_cp(("parallel",)), name="rmsnorm")(x2d, g.reshape(1, d))


def _matmul_kernel(x_ref, w_ref, o_ref):
    o_ref[...] = jnp.dot(x_ref[...], w_ref[...], preferred_element_type=F32)


def _matmul(x, w, tn, name):
    m, k = x.shape
    n = w.shape[1]
    tm = _tile(m, 1024)
    assert n % tn == 0
    return pl.pallas_call(
        _matmul_kernel, grid=(n // tn, m // tm),
        in_specs=[pl.BlockSpec((tm, k), lambda j, i: (i, 0)), pl.BlockSpec((k, tn), lambda j, i: (0, j))],
        out_specs=pl.BlockSpec((tm, tn), lambda j, i: (i, j)),
        out_shape=jax.ShapeDtypeStruct((m, n), F32),
        compiler_params=_cp(("parallel", "parallel")), name=name)(x, w)


def _conformer_kernel(a_ref, g_ref, ah_ref, gh_ref, prev_ref, w_ref, b_ref, lg_ref, lb_ref,
                      z_ref, st_ref, full_ref, y_ref, *, ktaps, tt):
    t = pl.program_id(1)
    u_halo = ah_ref[...] * _sigmoid(gh_ref[...])
    full_ref[0:CONV_HALO, :] = jnp.where(t == 0, prev_ref[...], u_halo)
    full_ref[CONV_HALO:CONV_HALO + tt, :] = a_ref[...] * _sigmoid(g_ref[...])
    width = y_ref.shape[1]

    def lane_chunk(c, carry):
        cs = pl.ds(pl.multiple_of(c * LANE, LANE), LANE)
        acc = jnp.broadcast_to(b_ref[:, cs], (tt, LANE))
        for j in range(ktaps):
            acc = acc + w_ref[j:j + 1, cs] * full_ref[pl.ds(CONV_HALO - (ktaps - 1) + j, tt), cs]
        y_ref[:, cs] = acc
        return carry

    lax.fori_loop(0, width // LANE, lane_chunk, 0)
    y = y_ref[...]
    xc = y - jnp.mean(y, axis=-1, keepdims=True)
    yn = xc * lax.rsqrt(jnp.mean(xc * xc, axis=-1, keepdims=True) + EPS) * lg_ref[...] + lb_ref[...]
    z_ref[...] = (yn * _sigmoid(yn)).astype(z_ref.dtype)
    st_ref[...] = full_ref[tt:tt + CONV_HALO, :]


def _conformer(h3, prev, conv_w, conv_b, ln_g, ln_b):
    b, t, _ = h3.shape
    ktaps = conv_w.shape[0]
    assert ktaps - 1 <= CONV_HALO
    tt = _tile(t, 256)
    nt = t // tt
    if t >= CONV_HALO:
        assert tt % CONV_HALO == 0
        halo_src = h3
        ratio = tt // CONV_HALO
        halo_a = lambda i, j: (i, jnp.maximum(j * ratio - 1, 0), OFF_A // W_A)
        halo_g = lambda i, j: (i, jnp.maximum(j * ratio - 1, 0), OFF_GT // W_A)
    else:
        assert nt == 1
        halo_src = jnp.zeros((b, CONV_HALO, W_A), F32)
        halo_a = lambda i, j: (i, 0, 0)
        halo_g = halo_a
    row = lambda v: v.reshape(1, W_A)
    z, st = pl.pallas_call(
        functools.partial(_conformer_kernel, ktaps=ktaps, tt=tt), grid=(b, nt),
        in_specs=[pl.BlockSpec((None, tt, W_A), lambda i, j: (i, j, OFF_A // W_A)),
                  pl.BlockSpec((None, tt, W_A), lambda i, j: (i, j, OFF_GT // W_A)),
                  pl.BlockSpec((None, CONV_HALO, W_A), halo_a),
                  pl.BlockSpec((None, CONV_HALO, W_A), halo_g),
                  pl.BlockSpec((None, CONV_HALO, W_A), lambda i, j: (i, 0, 0)),
                  pl.BlockSpec((ktaps, W_A), lambda i, j: (0, 0)),
                  pl.BlockSpec((1, W_A), lambda i, j: (0, 0)),
                  pl.BlockSpec((1, W_A), lambda i, j: (0, 0)),
                  pl.BlockSpec((1, W_A), lambda i, j: (0, 0))],
        out_specs=[pl.BlockSpec((None, tt, W_A), lambda i, j: (i, j, 0)),
                   pl.BlockSpec((None, CONV_HALO, W_A), lambda i, j: (i, 0, 0))],
        out_shape=[jax.ShapeDtypeStruct((b, t, W_A), BF16), jax.ShapeDtypeStruct((b, CONV_HALO, W_A), F32)],
        scratch_shapes=[pltpu.VMEM((CONV_HALO + tt, W_A), F32), pltpu.VMEM((tt, W_A), F32)],
        compiler_params=_cp(("parallel", "arbitrary")), name="conformer",
    )(h3, h3, halo_src, halo_src, prev, conv_w, row(conv_b), row(ln_g), row(ln_b))
    return z, st[:, CONV_HALO - (ktaps - 1):]


def _blockdiag_cmp_w(cw):
    eye = jnp.eye(KVH, dtype=cw.dtype)
    bd = jnp.einsum('klde,gh->klgdhe', cw, eye)
    return bd.reshape(2, L_CMP * KVH * HD, KVH * HD).astype(BF16)


def _tile_pe(pe):
    return jnp.broadcast_to(pe[:, :, None, :], (2, L_CMP, KVH, HD)).reshape(2, 1, L_CMP * KVH * HD)


def _compress_kernel(x_ref, pe_ref, w_ref, o_ref):
    x = (x_ref[...] + pe_ref[...]).astype(BF16)
    o_ref[...] = jnp.dot(x, w_ref[...], preferred_element_type=F32)


def _compress(x, pe_t, bd):
    _, r, kdim = x.shape
    n = KVH * HD
    tr = _tile(r, 256)
    return pl.pallas_call(
        _compress_kernel, grid=(2, r // tr),
        in_specs=[pl.BlockSpec((None, tr, kdim), lambda s, i: (s, i, 0)),
                  pl.BlockSpec((None, 1, kdim), lambda s, i: (s, 0, 0)),
                  pl.BlockSpec((None, kdim, n), lambda s, i: (s, 0, 0))],
        out_specs=pl.BlockSpec((None, tr, n), lambda s, i: (s, i, 0)),
        out_shape=jax.ShapeDtypeStruct((2, r, n), F32),
        compiler_params=_cp(("parallel", "parallel")), name="nsa_compress")(x, pe_t, bd)


def _even_odd(x, axis):
    n = x.shape[axis]
    y = x.reshape(x.shape[:axis] + (n // 2, 2) + x.shape[axis + 1:])
    return jnp.swapaxes(y, axis, axis + 1).reshape(x.shape)


def _rank_select_rows(score, k):
    n = score.shape[0]
    jrow = lax.broadcasted_iota(jnp.int32, score.shape, 0)
    cnt = jnp.zeros(score.shape, F32)
    for i in range(n):
        si = score[i:i + 1, :]
        ge = jnp.where(si >= score, 1.0, 0.0)
        gt = jnp.where(si > score, 1.0, 0.0)
        cnt = cnt + jnp.where(jrow > i, ge, gt)
    return jnp.where(cnt < k, 1.0, 0.0)


def _topk_lanes(score, k):
    n = score.shape[-1]
    cidx = lax.broadcasted_iota(jnp.int32, score.shape, score.ndim - 1).astype(F32)
    sel = jnp.zeros(score.shape, F32)
    s = score
    for _ in range(k):
        m = jnp.max(s, axis=-1, keepdims=True)
        imin = jnp.min(jnp.where(s == m, cidx, float(n)), axis=-1, keepdims=True)
        hit = cidx == imin
        sel = jnp.where(hit, 1.0, sel)
        s = jnp.where(hit, -jnp.inf, s)
    return sel


_NT = (((1,), (1,)), ((), ()))
_TN = (((0,), (0,)), ((), ()))


def _ones_lanes(v):
    return jnp.concatenate([v, jnp.ones(v.shape[:-1] + (LANE - v.shape[-1],), v.dtype)], axis=-1)


def _flash_update(m_prev, acc_prev, s, v_aug, nt=False):
    m_new = jnp.maximum(m_prev, jnp.max(s, axis=1, keepdims=True))
    alpha = jnp.exp(m_prev - m_new)
    p = jnp.exp(s - m_new[:, 0:1]).astype(BF16)
    if nt:
        pv = lax.dot_general(p, v_aug, _NT, preferred_element_type=F32)
    else:
        pv = jnp.dot(p, v_aug, preferred_element_type=F32)
    return m_new, acc_prev * alpha + pv


def _flash_step(j, s, v_aug, m_sc, acc_sc, nt=False):
    m_sc[j], acc_sc[j] = _flash_update(m_sc[j], acc_sc[j], s, v_aug, nt)


def _flash_reset(m_sc, acc_sc):
    m_sc[...] = jnp.full(m_sc.shape, NEG, F32)
    acc_sc[...] = jnp.zeros(acc_sc.shape, F32)


def _flash_out(acc):
    return acc[:, 0:HD] / acc[:, HD:2 * HD]


def _nsa_prompt_kernel(q_ref, bg_ref, kc_ref, vc_ref, ks_ref, vs_ref, kw_ref, vw_ref, ex_ref, o_ref,
                       q_sc, selm_sc, m_sc, acc_sc, win_sc, bias_sc, *, tq, tk, wk, nb, nsb):
    t0 = pl.program_id(2) * tq
    q = (q_ref[...] * SCALE).astype(BF16)
    for j in range(HPG):
        q_sc[j] = q[:, j * HD:(j + 1) * HD]
    kc = kc_ref[...]
    vc = vc_ref[...]

    half = nb // 2
    row = lax.broadcasted_iota(jnp.int32, (nb, tq), 0)
    tpos = t0 + lax.broadcasted_iota(jnp.int32, (nb, tq), 1)
    blk = jnp.where(row < half, 2 * row, 2 * (row - half) + 1)
    cvalid = (blk + 1) * L_CMP - 1 <= tpos
    sts = [lax.dot_general(kc, q_sc[j], _NT, preferred_element_type=F32) for j in range(HPG)]
    pcs = []
    for j in range(HPG):
        sm = jnp.where(cvalid, sts[j], NEG)
        e = jnp.exp(sm - jnp.max(sm, axis=0, keepdims=True))
        pcs.append(jnp.where(cvalid, e / jnp.sum(e, axis=0, keepdims=True), 0.0))
    o_cmp = [lax.dot_general(pcs[j].astype(BF16), vc, _TN, preferred_element_type=F32) for j in range(HPG)]
    pg = pcs[0]
    for j in range(1, HPG):
        pg = pg + pcs[j]

    ps = pg[:half] + pg[half:]
    jrow = lax.broadcasted_iota(jnp.int32, (nsb, tq), 0)
    tp = t0 + lax.broadcasted_iota(jnp.int32, (nsb, tq), 1)
    cur = tp // L_SEL
    forced_ps = jnp.where(jrow == 0, BIG, jnp.where(jrow == cur, BIG, jnp.where(jrow == cur - 1, BIG, ps)))
    score = jnp.where(jrow * L_SEL <= tp, forced_ps, -2.0 * BIG)
    sel = _rank_select_rows(score, min(N_SEL, nsb))
    sel = jnp.concatenate([sel, jnp.zeros((NSB_PAD - nsb, tq), F32)], axis=0)
    selm_sc[...] = jnp.transpose(sel).astype(BF16)

    _flash_reset(m_sc, acc_sc)
    nrb = tq // ROW_BLOCK

    def sel_chunk(c, carry):
        k0 = pl.multiple_of(c * tk, tk)
        chosen = jnp.dot(selm_sc[...], ex_ref[:, pl.ds(k0, tk)], preferred_element_type=F32)
        qpos = t0 + lax.broadcasted_iota(jnp.int32, (tq, tk), 0)
        kpos = k0 + lax.broadcasted_iota(jnp.int32, (tq, tk), 1)
        bias_sc[:, 0:tk] = jnp.where(kpos <= qpos, (chosen - 1.0) * (-NEG), NEG)

        def rows(rb, carry2):
            rs = pl.ds(pl.multiple_of(rb * ROW_BLOCK, ROW_BLOCK), ROW_BLOCK)
            ks = ks_ref[pl.ds(k0, tk), :]
            vs = vs_ref[pl.ds(k0, tk), :]
            bias = bias_sc[rs, 0:tk]
            heads = range(HPG)
            ss = [(lax.dot_general(q_sc[j, rs, :], ks, _NT, preferred_element_type=F32) + bias).astype(BF16)
                  for j in heads]
            m_prev = [m_sc[j, rs, :] for j in heads]
            m_new = [jnp.maximum(m_prev[j], jnp.max(ss[j], axis=1, keepdims=True).astype(F32)) for j in heads]
            ps = [jnp.exp(ss[j] - m_new[j][:, 0:1].astype(BF16)) for j in heads]
            pvs = [jnp.dot(ps[j], vs, preferred_element_type=F32) for j in heads]
            for j in heads:
                acc_sc[j, rs, :] = acc_sc[j, rs, :] * jnp.exp(m_prev[j] - m_new[j]) + pvs[j]
                m_sc[j, rs, :] = m_new[j]
            return carry2

        lax.fori_loop(0, nrb, rows, 0)
        return carry

    lax.fori_loop(0, (t0 + tq + tk - 1) // tk, sel_chunk, 0)

    w0 = pl.multiple_of(jnp.maximum(t0 - WINDOW, 0), LANE)
    d = (t0 + lax.broadcasted_iota(jnp.int32, (tq, wk), 0)) - (w0 + lax.broadcasted_iota(jnp.int32, (tq, wk), 1))
    bias_sc[:, 0:wk] = jnp.where(d >= 0, jnp.where(d < WINDOW, 0.0, NEG), NEG)

    def win_rows(rb, carry):
        rs = pl.ds(pl.multiple_of(rb * ROW_BLOCK, ROW_BLOCK), ROW_BLOCK)
        kw = kw_ref[pl.ds(w0, wk), :]
        vw = vw_ref[pl.ds(w0, wk), :]
        bias = bias_sc[rs, 0:wk]
        heads = range(HPG)
        ss = [(lax.dot_general(q_sc[j, rs, :], kw, _NT, preferred_element_type=F32) + bias).astype(BF16)
              for j in heads]
        ps = [jnp.exp(ss[j] - jnp.max(ss[j], axis=1, keepdims=True)) for j in heads]
        pvs = [jnp.dot(ps[j], vw, preferred_element_type=F32) for j in heads]
        for j in heads:
            win_sc[j, rs, :] = pvs[j]
        return carry

    lax.fori_loop(0, nrb, win_rows, 0)

    sg = _sigmoid(bg_ref[...])
    outs = []
    for j in range(HPG):
        outs.append(sg[:, 3 * j:3 * j + 1] * o_cmp[j] + sg[:, 3 * j + 1:3 * j + 2] * _flash_out(acc_sc[j])
                    + sg[:, 3 * j + 2:3 * j + 3] * _flash_out(win_sc[j]))
    o_ref[...] = jnp.concatenate(outs, axis=1).astype(o_ref.dtype)


def _nsa_prompt(h3, pe_t, bd):
    b, t, _ = h3.shape
    tq, tk = 256, 512
    wk = WINDOW + tq
    assert t % tk == 0 and t % L_SEL == 0 and t >= wk and WINDOW % LANE == 0 and tq % ROW_BLOCK == 0
    nb = t // L_CMP
    nsb = t // L_SEL
    assert nsb <= NSB_PAD and nb == 2 * nsb
    kv = h3[:, :, OFF_KV:OFF_KV + 6 * KVH * HD]
    xc = jnp.stack([kv[:, :, 0:KVH * HD].reshape(b * nb, L_CMP * KVH * HD),
                    kv[:, :, KVH * HD:2 * KVH * HD].reshape(b * nb, L_CMP * KVH * HD)])
    cmp = _compress(xc, pe_t, bd).reshape(2, b, nb, KVH, HD)
    cmp = jnp.transpose(_even_odd(cmp, 2), (0, 1, 3, 2, 4)).astype(BF16)
    kvh = jnp.transpose(kv[:, :, 2 * KVH * HD:].reshape(b, t, 4, KVH, HD), (2, 0, 3, 1, 4)).astype(BF16)
    expand = jnp.asarray(np.arange(NSB_PAD)[:, None] == np.arange(t)[None, :] // L_SEL, dtype=BF16)
    qcol = OFF_Q // (HPG * HD)
    bgcol = OFF_BG // LANE
    head = lambda w: pl.BlockSpec((None, None, t, w), lambda i, g, n: (i, g, 0, 0))
    return pl.pallas_call(
        functools.partial(_nsa_prompt_kernel, tq=tq, tk=tk, wk=wk, nb=nb, nsb=nsb), grid=(b, KVH, t // tq),
        in_specs=[pl.BlockSpec((None, tq, HPG * HD), lambda i, g, n: (i, n, qcol + g)),
                  pl.BlockSpec((None, tq, LANE), lambda i, g, n: (i, n, bgcol + g)),
                  pl.BlockSpec((None, None, nb, HD), lambda i, g, n: (i, g, 0, 0)),
                  pl.BlockSpec((None, None, nb, HD), lambda i, g, n: (i, g, 0, 0)),
                  head(HD), head(LANE), head(HD), head(LANE),
                  pl.BlockSpec((NSB_PAD, t), lambda i, g, n: (0, 0))],
        out_specs=pl.BlockSpec((None, tq, HPG * HD), lambda i, g, n: (i, n, g)),
        out_shape=jax.ShapeDtypeStruct((b, t, H_B * HD), BF16),
        scratch_shapes=[pltpu.VMEM((HPG, tq, HD), BF16), pltpu.VMEM((tq, NSB_PAD), BF16),
                        pltpu.VMEM((HPG, tq, LANE), F32), pltpu.VMEM((HPG, tq, LANE), F32),
                        pltpu.VMEM((HPG, tq, LANE), F32), pltpu.VMEM((tq, max(tk, wk)), F32)],
        compiler_params=_cp(("parallel", "parallel", "arbitrary")), name="nsa_prompt",
    )(h3, h3, cmp[0], cmp[1], kvh[0], _ones_lanes(kvh[1]), kvh[2], _ones_lanes(kvh[3]), expand)


def _cmp_perm(page):
    per = page // L_CMP
    m = np.zeros((2 * page, 2 * page), np.float32)
    for l in range(L_CMP):
        for p in range(2):
            for n in range(per):
                m[l * 2 * per + p * per + n, p * page + n * L_CMP + l] = 1.0
    return jnp.asarray(m, dtype=BF16)


def _cmp_past_kernel(pt_ref, *refs, n_steps, per):
    pages = refs[:PAGES_PER_STEP]
    pe_ref, perm_ref, w_ref, o_ref, x_sc = refs[PAGES_PER_STEP:]
    pg = pl.program_id(1)
    gd = KVH * HD
    page = pages[0].shape[-1]
    rows = PAGES_PER_STEP * per
    grp = 2 * per
    for s in range(2):
        moved = []
        for pr in range(PAGES_PER_STEP // 2):
            xt = jnp.concatenate([pages[2 * pr][s].reshape(gd, page), pages[2 * pr + 1][s].reshape(gd, page)], axis=1)
            xt = (xt + pe_ref[s]).astype(BF16)
            moved.append(lax.dot_general(perm_ref[...], xt, _NT, preferred_element_type=F32))
        for l in range(L_CMP):
            slab = jnp.concatenate([m[l * grp:(l + 1) * grp] for m in moved], axis=0)
            x_sc[s, l, pl.ds(pl.multiple_of(pg * rows, rows), rows), :] = slab.astype(BF16)

    @pl.when(pg == n_steps - 1)
    def _():
        o_ref[...] = jnp.zeros(o_ref.shape, F32)

        def body(l, carry):
            for s in range(2):
                o_ref[s] += jnp.dot(x_sc[s, l], w_ref[s, l], preferred_element_type=F32)
            return carry

        lax.fori_loop(0, L_CMP, body, 0)


def _cmp_past(cache_t, page_table, layer, pe, bd):
    b, n_pages = page_table.shape
    page = cache_t.shape[-1]
    per = page // L_CMP
    gd = KVH * HD
    assert n_pages % PAGES_PER_STEP == 0 and PAGES_PER_STEP % 2 == 0 and 2 * per == 8
    n_steps = n_pages // PAGES_PER_STEP
    nbp = n_pages * per
    pe_cols = jnp.tile(jnp.swapaxes(pe, 1, 2), (1, KVH, 2 * per))
    specs = []
    for slot in range(PAGES_PER_STEP):
        def index_map(i, pg, pt, slot=slot):
            return (layer, pt[i, pg * PAGES_PER_STEP + slot], 0, 0, 0, 0)
        specs.append(pl.BlockSpec((None, None, 2, KVH, HD, page), index_map))
    grid_spec = pltpu.PrefetchScalarGridSpec(
        num_scalar_prefetch=1, grid=(b, n_steps),
        in_specs=specs + [pl.BlockSpec((2, gd, 2 * page), lambda i, pg, pt: (0, 0, 0)),
                          pl.BlockSpec((2 * page, 2 * page), lambda i, pg, pt: (0, 0)),
                          pl.BlockSpec((2, L_CMP, gd, gd), lambda i, pg, pt: (0, 0, 0, 0))],
        out_specs=pl.BlockSpec((None, 2, nbp, gd), lambda i, pg, pt: (i, 0, 0, 0)),
        scratch_shapes=[pltpu.VMEM((2, L_CMP, nbp, gd), BF16)])
    return pl.pallas_call(
        functools.partial(_cmp_past_kernel, n_steps=n_steps, per=per), grid_spec=grid_spec,
        out_shape=jax.ShapeDtypeStruct((b, 2, nbp, gd), F32),
        compiler_params=_cp(("parallel", "arbitrary")), name="nsa_cmp_past",
    )(page_table, *([cache_t] * PAGES_PER_STEP), pe_cols, _cmp_perm(page), bd.reshape(2, L_CMP, gd, gd))


def _stack_heads(x, g):
    return jnp.concatenate([x[:, (g * HPG + j) * HD:(g * HPG + j + 1) * HD] for j in range(HPG)], axis=0)


def _sample_cmp_win_kernel(q_ref, bg_ref, kc_ref, vc_ref, wp_ref, wn_ref, ocw_ref, sel_ref,
                           *, t, past, nbp, nsb, sel_pad):
    q = (q_ref[...] * SCALE).astype(BF16)
    sg = _sigmoid(bg_ref[...])
    rows = HPG * t
    half = nbp // 2
    wb = wp_ref.shape[0]
    gw = KVH * HD

    col = lax.broadcasted_iota(jnp.int32, (rows, nbp), 1)
    tq = lax.broadcasted_iota(jnp.int32, (rows, nbp), 0) % t
    blk = jnp.where(col < half, 2 * col, 2 * (col - half) + 1)
    cvalid = (blk + 1) * L_CMP - 1 <= past + tq

    wi = lax.broadcasted_iota(jnp.int32, (rows, wb), 1)
    wt = lax.broadcasted_iota(jnp.int32, (rows, wb), 0) % t
    d_prev = wt + wb - wi
    ni = lax.broadcasted_iota(jnp.int32, (rows, t), 1)
    nt = lax.broadcasted_iota(jnp.int32, (rows, t), 0) % t
    d_new = nt - ni

    ps_all = []
    outs = [None] * H_B
    for g in range(KVH):
        qg = _stack_heads(q, g)
        kc = kc_ref[:, g * HD:(g + 1) * HD].astype(BF16)
        vc = vc_ref[:, g * HD:(g + 1) * HD].astype(BF16)
        s = lax.dot_general(qg, kc, _NT, preferred_element_type=F32)
        sm = jnp.where(cvalid, s, NEG)
        e = jnp.exp(sm - jnp.max(sm, axis=1, keepdims=True))
        p = jnp.where(cvalid, e / jnp.sum(e, axis=1, keepdims=True), 0.0)
        o_cmp = jnp.dot(p.astype(BF16), vc, preferred_element_type=F32)
        pgrp = p[0:t]
        for j in range(1, HPG):
            pgrp = pgrp + p[j * t:(j + 1) * t]
        ps_all.append(pgrp[:, :half] + pgrp[:, half:])

        kp = wp_ref[:, g * HD:(g + 1) * HD].astype(BF16)
        vp = wp_ref[:, gw + g * HD:gw + (g + 1) * HD].astype(BF16)
        kn = wn_ref[:, g * HD:(g + 1) * HD].astype(BF16)
        vn = wn_ref[:, gw + g * HD:gw + (g + 1) * HD].astype(BF16)
        s1 = lax.dot_general(qg, kp, _NT, preferred_element_type=F32)
        s1 = jnp.where(d_prev >= 0, jnp.where(d_prev < WINDOW, s1, NEG), NEG)
        s2 = lax.dot_general(qg, kn, _NT, preferred_element_type=F32)
        s2 = jnp.where(d_new >= 0, jnp.where(d_new < WINDOW, s2, NEG), NEG)
        mx = jnp.maximum(jnp.max(s1, axis=1, keepdims=True), jnp.max(s2, axis=1, keepdims=True))
        e1 = jnp.exp(s1 - mx)
        e2 = jnp.exp(s2 - mx)
        den = jnp.sum(e1, axis=1, keepdims=True) + jnp.sum(e2, axis=1, keepdims=True)
        o_win = (jnp.dot(e1.astype(BF16), vp, preferred_element_type=F32)
                 + jnp.dot(e2.astype(BF16), vn, preferred_element_type=F32)) / den
        for j in range(HPG):
            c = g * LANE + 3 * j
            outs[g * HPG + j] = (sg[:, c:c + 1] * o_cmp[j * t:(j + 1) * t]
                                 + sg[:, c + 2:c + 3] * o_win[j * t:(j + 1) * t])
    ocw_ref[...] = jnp.concatenate(outs, axis=1)

    ps = jnp.concatenate(ps_all, axis=0)
    ps = jnp.concatenate([ps, jnp.zeros((KVH * t, sel_pad - half), F32)], axis=1)
    jb = lax.broadcasted_iota(jnp.int32, (KVH * t, sel_pad), 1)
    qp = past + lax.broadcasted_iota(jnp.int32, (KVH * t, sel_pad), 0) % t
    cur = qp // L_SEL
    forced_ps = jnp.where(jb == 0, BIG, jnp.where(jb == cur, BIG, jnp.where(jb == cur - 1, BIG, ps)))
    score = jnp.where(jb * L_SEL <= qp, forced_ps, -2.0 * BIG)
    score = jnp.where(jb < nsb, score, -jnp.inf)
    sel_ref[...] = _topk_lanes(score, min(N_SEL, nsb))


def _sample_sel_kernel(pt_ref, *refs, t, n_steps):
    pages = refs[:PAGES_PER_STEP]
    q_ref, bg_ref, selp_ref, seln_ref, ex_ref, kvn_ref, ocw_ref, o_ref, q_sc, m_sc, acc_sc = refs[PAGES_PER_STEP:]
    pg = pl.program_id(1)
    rows = HPG * t
    gw = KVH * HD
    keys = PAGES_PER_STEP * pages[0].shape[-1]

    @pl.when(pg == 0)
    def _():
        q = (q_ref[...] * SCALE).astype(BF16)
        for g in range(KVH):
            q_sc[g] = _stack_heads(q, g)
        _flash_reset(m_sc, acc_sc)

    def per_head_rows(x):
        return jnp.concatenate([x[g * t:(g + 1) * t] for g in range(KVH) for _ in range(HPG)], axis=0)

    chosen = jnp.dot(per_head_rows(selp_ref[...]).astype(BF16), ex_ref[...], preferred_element_type=F32)
    ones = jnp.ones((LANE - HD, keys), BF16)
    bias = (chosen - 1.0) * (-NEG)
    groups = range(KVH)
    ss = []
    for g in groups:
        kt = jnp.concatenate([p[0, g] for p in pages], axis=1).astype(BF16)
        ss.append(jnp.dot(q_sc[g], kt, preferred_element_type=F32))
    s = jnp.concatenate(ss, axis=0) + bias
    m_prev = m_sc[...]
    m_new = jnp.maximum(m_prev, jnp.max(s, axis=1, keepdims=True))
    p = jnp.exp(s - m_new[:, 0:1]).astype(BF16)
    pvs = []
    for g in groups:
        vt = jnp.concatenate([pp[1, g] for pp in pages], axis=1).astype(BF16)
        pvs.append(lax.dot_general(p[g * rows:(g + 1) * rows], jnp.concatenate([vt, ones], axis=0), _NT,
                                   preferred_element_type=F32))
    acc_sc[...] = acc_sc[...] * jnp.exp(m_prev - m_new) + jnp.concatenate(pvs, axis=0)
    m_sc[...] = m_new

    @pl.when(pg == n_steps - 1)
    def _():
        sg = _sigmoid(bg_ref[...])
        ni = lax.broadcasted_iota(jnp.int32, (rows, t), 1)
        nt = lax.broadcasted_iota(jnp.int32, (rows, t), 0) % t
        new_chosen = per_head_rows(seln_ref[...])[:, 0:1] > 0.5
        outs = []
        for g in range(KVH):
            rs = slice(g * rows, (g + 1) * rows)
            kn = kvn_ref[:, g * HD:(g + 1) * HD].astype(BF16)
            vn = _ones_lanes(kvn_ref[:, gw + g * HD:gw + (g + 1) * HD].astype(BF16))
            s = lax.dot_general(q_sc[g], kn, _NT, preferred_element_type=F32)
            s = jnp.where(ni <= nt, jnp.where(new_chosen[rs], s, NEG), NEG)
            _flash_step(rs, s, vn, m_sc, acc_sc)
            o_sel = _flash_out(acc_sc[rs])
            for j in range(HPG):
                c = g * LANE + 3 * j + 1
                outs.append(sg[:, c:c + 1] * o_sel[j * t:(j + 1) * t])
        o_ref[...] = (ocw_ref[...] + jnp.concatenate(outs, axis=1)).astype(o_ref.dtype)


def _nsa_sample(h3, cache_t, page_table, layer, win_prev, pe, bd):
    b, t, _ = h3.shape
    n_pages = page_table.shape[1]
    page = cache_t.shape[-1]
    past = n_pages * page
    assert t < L_CMP and t <= L_SEL and past % L_SEL == 0 and t % 8 == 0
    nbp = past // L_CMP
    n_past_blk = past // L_SEL
    nsb = -(-(past + t) // L_SEL)
    assert nsb == n_past_blk + 1
    sel_pad = -(-nsb // LANE) * LANE
    gw = KVH * HD
    cmp = _even_odd(_cmp_past(cache_t, page_table, layer, pe, bd), 2)
    wb = win_prev.shape[1]
    wp = win_prev.reshape(b, wb, 2 * gw)
    ocw, sel = pl.pallas_call(
        functools.partial(_sample_cmp_win_kernel, t=t, past=past, nbp=nbp, nsb=nsb, sel_pad=sel_pad),
        grid=(b,),
        in_specs=[pl.BlockSpec((None, t, H_B * HD), lambda i: (i, 0, OFF_Q // (H_B * HD))),
                  pl.BlockSpec((None, t, KVH * LANE), lambda i: (i, 0, OFF_BG // (KVH * LANE))),
                  pl.BlockSpec((None, None, nbp, gw), lambda i: (i, 0, 0, 0)),
                  pl.BlockSpec((None, None, nbp, gw), lambda i: (i, 1, 0, 0)),
                  pl.BlockSpec((None, wb, 2 * gw), lambda i: (i, 0, 0)),
                  pl.BlockSpec((None, t, 2 * gw), lambda i: (i, 0, (OFF_KV + 4 * gw) // (2 * gw)))],
        out_specs=[pl.BlockSpec((None, t, H_B * HD), lambda i: (i, 0, 0)),
                   pl.BlockSpec((None, KVH * t, sel_pad), lambda i: (i, 0, 0))],
        out_shape=[jax.ShapeDtypeStruct((b, t, H_B * HD), F32),
                   jax.ShapeDtypeStruct((b, KVH * t, sel_pad), F32)],
        compiler_params=_cp(("parallel",)), name="nsa_sample_cmp_win",
    )(h3, h3, cmp, cmp, wp, h3)

    n_steps = n_pages // PAGES_PER_STEP
    keys = PAGES_PER_STEP * page
    bps = keys // L_SEL
    assert keys % L_SEL == 0 and bps <= LANE
    lane_pad = lambda x: jnp.pad(x, [(0, 0)] * (x.ndim - 1) + [(0, LANE - x.shape[-1])])
    sel_past = lane_pad(jnp.swapaxes(sel[:, :, :n_past_blk].reshape(b, KVH * t, n_steps, bps), 1, 2))
    sel_new = lane_pad(sel[:, :, n_past_blk:n_past_blk + 1])
    expand = jnp.asarray(np.arange(LANE)[:, None] == np.arange(keys)[None, :] // L_SEL, dtype=BF16)
    specs = []
    for slot in range(PAGES_PER_STEP):
        def index_map(i, pg, pt, slot=slot):
            return (layer, pt[i, pg * PAGES_PER_STEP + slot], 1, 0, 0, 0)
        specs.append(pl.BlockSpec((None, None, 2, KVH, HD, page), index_map))
    nrow = KVH * HPG * t
    grid_spec = pltpu.PrefetchScalarGridSpec(
        num_scalar_prefetch=1, grid=(b, n_steps),
        in_specs=specs + [
            pl.BlockSpec((None, t, H_B * HD), lambda i, pg, pt: (i, 0, OFF_Q // (H_B * HD))),
            pl.BlockSpec((None, t, KVH * LANE), lambda i, pg, pt: (i, 0, OFF_BG // (KVH * LANE))),
            pl.BlockSpec((None, None, KVH * t, LANE), lambda i, pg, pt: (i, pg, 0, 0)),
            pl.BlockSpec((None, KVH * t, LANE), lambda i, pg, pt: (i, 0, 0)),
            pl.BlockSpec((LANE, keys), lambda i, pg, pt: (0, 0)),
            pl.BlockSpec((None, t, 2 * gw), lambda i, pg, pt: (i, 0, (OFF_KV + 2 * gw) // (2 * gw))),
            pl.BlockSpec((None, t, H_B * HD), lambda i, pg, pt: (i, 0, 0))],
        out_specs=pl.BlockSpec((None, t, H_B * HD), lambda i, pg, pt: (i, 0, 0)),
        scratch_shapes=[pltpu.VMEM((KVH, HPG * t, HD), BF16), pltpu.VMEM((nrow, LANE), F32),
                        pltpu.VMEM((nrow, LANE), F32)])
    return pl.pallas_call(
        functools.partial(_sample_sel_kernel, t=t, n_steps=n_steps),
        grid_spec=grid_spec, out_shape=jax.ShapeDtypeStruct((b, t, H_B * HD), BF16),
        compiler_params=_cp(("parallel", "arbitrary")), name="nsa_sample_sel",
    )(page_table, *([cache_t] * PAGES_PER_STEP), h3, h3, sel_past, sel_new, expand, h3, ocw)


def _hgrn_lmats(c):
    nlev = int(math.log2(c))
    assert 1 << nlev == c
    r = np.arange(c)[:, None]
    i = np.arange(c)[None, :]
    mats = [i <= r, i > r]
    pairs = []
    for lev in range(nlev):
        blk = c >> lev
        mid = (r // blk) * blk + blk // 2
        mats.append(np.where(r >= mid, (i >= mid) & (i <= r), (i > r) & (i < mid)))
        pairs.append((r // blk == i // blk) & (r % blk >= blk // 2) & (i % blk < blk // 2))
    return (jnp.asarray(np.concatenate(mats, axis=0).astype(np.float32), dtype=BF16),
            jnp.asarray(np.stack(pairs).astype(np.float32)))


HGRN_HEADS_PER_STEP = 4


def _hgrn_kernel(cq_ref, cf_ref, ci_ref, cg_ref, lbl_ref, ng_ref, s0_ref, lm_ref, pm_ref, z_ref, sn_ref, st_sc,
                 *, c, nch, layer, nlev):
    tstep = pl.program_id(2)
    nh = HGRN_HEADS_PER_STEP

    @pl.when(tstep == 0)
    def _():
        for hh in range(nh):
            st_sc[hh] = jnp.transpose(s0_ref[hh])

    lg = lbl_ref[...]
    e = jnp.exp(lg - jnp.max(lg, axis=0, keepdims=True))
    p = e / jnp.sum(e, axis=0, keepdims=True)
    cs = p[0:1]
    for i in range(1, layer + 1):
        cs = cs + p[i:i + 1]
    lb_all = cs - p[layer:layer + 1]
    lm = lm_ref[...]
    cp = max(c, LANE)

    def chunk(ci, carry):
        rs = pl.ds(pl.multiple_of(ci * c, c), c)
        heads = range(nh)
        hsl = [slice(hh * DK_C, (hh + 1) * DK_C) for hh in heads]
        fp = [cf_ref[rs, hs] for hs in hsl]
        cq = [cq_ref[rs, hs] for hs in hsl]
        v = [ci_ref[rs, hs] for hs in hsl]
        kk, qq, l2 = [], [], []
        for hh in heads:
            lb = lb_all[:, hsl[hh]]
            log_lb = jnp.log(jnp.maximum(lb, LB_TINY))
            log_sig = jnp.minimum(fp[hh], 0.0) - jnp.log(1.0 + jnp.exp(-jnp.abs(fp[hh])))
            x2 = jnp.log1p(-lb) + log_sig
            logf = jnp.where(lb > 0, jnp.maximum(log_lb, x2) + jnp.log(1.0 + jnp.exp(-jnp.abs(log_lb - x2))),
                             log_sig)
            kk.append((1.0 - lb) * _sigmoid(-fp[hh]))
            qq.append(cq[hh] * _sigmoid(cq[hh]))
            hi = logf.astype(BF16)
            l2.append(jnp.concatenate([hi, (logf - hi.astype(F32)).astype(BF16)], axis=1))
        d2 = [jnp.dot(lm, l2[hh], preferred_element_type=F32) for hh in heads]
        dd = [d[:, :DK_C] + d[:, DK_C:] for d in d2]
        st = [st_sc[hh] for hh in heads]
        o1 = [lax.dot_general((qq[hh] * jnp.exp(dd[hh][0:c])).astype(BF16), st[hh].astype(BF16), _NT,
                              preferred_element_type=F32) for hh in heads]
        al = []
        for lev in range(nlev):
            for hh in heads:
                ee = jnp.exp(dd[hh][(2 + lev) * c:(3 + lev) * c])
                al.append(lax.dot_general((qq[hh] * ee).astype(BF16), (kk[hh] * ee).astype(BF16), _NT,
                                          preferred_element_type=F32))
        amat = []
        for hh in heads:
            acc = al[hh] * pm_ref[0]
            for lev in range(1, nlev):
                acc = acc + al[lev * nh + hh] * pm_ref[lev]
            amat.append(acc.astype(BF16))
        o2 = [jnp.dot(amat[hh], v[hh].astype(BF16), preferred_element_type=F32) for hh in heads]
        upd = []
        for hh in heads:
            kd = kk[hh] * jnp.exp(dd[hh][c:2 * c])
            vp = v[hh]
            if cp > c:
                zpad = jnp.zeros((cp - c, DK_C), F32)
                kd = jnp.concatenate([kd, zpad], axis=0)
                vp = jnp.concatenate([vp, zpad], axis=0)
            upd.append(jnp.dot(jnp.transpose(vp).astype(BF16), kd.astype(BF16), preferred_element_type=F32))
        for hh in heads:
            st_sc[hh] = st[hh] * jnp.exp(dd[hh][c - 1:c, :]) + upd[hh]
            o = o1[hh] + o2[hh] + jnp.sum(qq[hh] * kk[hh], axis=1, keepdims=True) * v[hh]
            y = o * lax.rsqrt(jnp.mean(o * o, axis=1, keepdims=True) + EPS) * ng_ref[:, hsl[hh]]
            cg = cg_ref[rs, hsl[hh]]
            z_ref[rs, hsl[hh]] = (y * (cg * _sigmoid(cg))).astype(z_ref.dtype)
        return carry

    lax.fori_loop(0, nch, chunk, 0)

    @pl.when(tstep == pl.num_programs(2) - 1)
    def _():
        for hh in range(nh):
            sn_ref[hh] = jnp.transpose(st_sc[hh])


def _hgrn(h3, s0, lb_logits, norm_g, layer):
    b, t, _ = h3.shape
    c = 128 if t % 128 == 0 else t
    tt = _tile(t, 512)
    assert tt % c == 0
    nch = tt // c
    nlev = int(math.log2(c))
    lm, pm = _hgrn_lmats(c)
    depth = lb_logits.shape[0]
    nh = HGRN_HEADS_PER_STEP
    w = nh * DK_C
    assert H_C % nh == 0 and DK_C == DV_C
    col = lambda off: (lambda i, h, n: (i, n, off // w + h))
    z, sn = pl.pallas_call(
        functools.partial(_hgrn_kernel, c=c, nch=nch, layer=layer, nlev=nlev), grid=(b, H_C // nh, t // tt),
        in_specs=[pl.BlockSpec((None, tt, w), col(OFF_CQ)),
                  pl.BlockSpec((None, tt, w), col(OFF_CF)),
                  pl.BlockSpec((None, tt, w), col(OFF_CI)),
                  pl.BlockSpec((None, tt, w), col(OFF_CG)),
                  pl.BlockSpec((depth, w), lambda i, h, n: (0, h)),
                  pl.BlockSpec((1, w), lambda i, h, n: (0, h)),
                  pl.BlockSpec((None, nh, DK_C, DV_C), lambda i, h, n: (i, h, 0, 0)),
                  pl.BlockSpec(lm.shape, lambda i, h, n: (0, 0)),
                  pl.BlockSpec(pm.shape, lambda i, h, n: (0, 0, 0))],
        out_specs=[pl.BlockSpec((None, tt, w), lambda i, h, n: (i, n, h)),
                   pl.BlockSpec((None, nh, DK_C, DV_C), lambda i, h, n: (i, h, 0, 0))],
        out_shape=[jax.ShapeDtypeStruct((b, t, H_C * DV_C), BF16),
                   jax.ShapeDtypeStruct((b, H_C, DK_C, DV_C), F32)],
        scratch_shapes=[pltpu.VMEM((nh, DV_C, DK_C), F32)],
        compiler_params=_cp(("parallel", "parallel", "arbitrary")), name="hgrn2",
    )(h3, h3, h3, h3, lb_logits, norm_g.reshape(1, H_C * DV_C), s0, lm, pm)
    return z, sn


def _merge_kernel(za_ref, zb_ref, zc_ref, ga_ref, gb_ref, gc_ref, x_ref, wa_ref, wb_ref, wc_ref, wo_ref,
                  g2_ref, x1_ref, xn_ref):
    ya = jnp.dot(za_ref[...], wa_ref[...], preferred_element_type=F32)
    yb = jnp.dot(zb_ref[...], wb_ref[...], preferred_element_type=F32)
    yc = jnp.dot(zc_ref[...], wc_ref[...], preferred_element_type=F32)
    y = _sigmoid(ga_ref[...]) * ya + _sigmoid(gb_ref[...]) * yb + _sigmoid(gc_ref[...]) * yc
    x1 = x_ref[...] + jnp.dot(y.astype(BF16), wo_ref[...], preferred_element_type=F32)
    x1_ref[...] = x1
    xn = x1 * lax.rsqrt(jnp.mean(x1 * x1, axis=-1, keepdims=True) + EPS) * g2_ref[...]
    xn_ref[...] = xn.astype(xn_ref.dtype)


def _merge(za, zb, zc, h2, x2d, wa, wb, wc, wo, g2):
    m, d = x2d.shape
    tm = _tile(m, 256)
    act = lambda: pl.BlockSpec((tm, d), lambda i: (i, 0))
    gate = lambda k: pl.BlockSpec((tm, d), lambda i: (i, OFF_MG // d + k))
    wgt = lambda: pl.BlockSpec((d, d), lambda i: (0, 0))
    return pl.pallas_call(
        _merge_kernel, grid=(m // tm,),
        in_specs=[act(), act(), act(), gate(0), gate(1), gate(2), act(), wgt(), wgt(), wgt(), wgt(),
                  pl.BlockSpec((1, d), lambda i: (0, 0))],
        out_specs=[act(), act()],
        out_shape=[jax.ShapeDtypeStruct((m, d), F32), jax.ShapeDtypeStruct((m, d), BF16)],
        compiler_params=_cp(("parallel",)), name="merge",
    )(za, zb, zc, h2, h2, h2, x2d, wa, wb, wc, wo, g2.reshape(1, d))


def _ffn_kernel(ug_ref, uv_ref, hg_ref, hv_ref, pg_ref, pv_ref, cw_ref, wd_ref, x1_ref, gn_ref,
                x2_ref, xn_ref, st_ref, fg_sc, fv_sc, *, ktaps, tt):
    t = pl.program_id(1)
    dff = ug_ref.shape[1]
    fg_sc[0:FFN_HALO, :] = jnp.where(t == 0, pg_ref[...], hg_ref[...])
    fv_sc[0:FFN_HALO, :] = jnp.where(t == 0, pv_ref[...], hv_ref[...])
    fg_sc[FFN_HALO:FFN_HALO + tt, :] = ug_ref[...]
    fv_sc[FFN_HALO:FFN_HALO + tt, :] = uv_ref[...]
    gate = jnp.zeros((tt, dff), F32)
    val = jnp.zeros((tt, dff), F32)
    for j in range(ktaps):
        off = FFN_HALO - (ktaps - 1) + j
        gate = gate + cw_ref[j:j + 1, 0:dff] * fg_sc[pl.ds(off, tt), :]
        val = val + cw_ref[j:j + 1, dff:2 * dff] * fv_sc[pl.ds(off, tt), :]
    act = (gate * _sigmoid(gate) * val).astype(BF16)
    x2 = x1_ref[...] + jnp.dot(act, wd_ref[...], preferred_element_type=F32)
    x2_ref[...] = x2
    xn = x2 * lax.rsqrt(jnp.mean(x2 * x2, axis=-1, keepdims=True) + EPS) * gn_ref[...]
    xn_ref[...] = xn.astype(xn_ref.dtype)
    st_ref[:, 0:dff] = fg_sc[tt:tt + FFN_HALO, :]
    st_ref[:, dff:2 * dff] = fv_sc[tt:tt + FFN_HALO, :]


def _ffn_tail(u3, prev, conv_w, w_down, x1_3, g_next, xn_dtype):
    b, t, two_dff = u3.shape
    dff = two_dff // 2
    d = x1_3.shape[2]
    ktaps = conv_w.shape[0]
    assert ktaps - 1 <= FFN_HALO
    tt = _tile(t, 256)
    nt = t // tt
    ratio = tt // FFN_HALO
    halo = lambda k: (lambda i, j: (i, jnp.maximum(j * ratio - 1, 0), k))
    x2, xn, st = pl.pallas_call(
        functools.partial(_ffn_kernel, ktaps=ktaps, tt=tt), grid=(b, nt),
        in_specs=[pl.BlockSpec((None, tt, dff), lambda i, j: (i, j, 0)),
                  pl.BlockSpec((None, tt, dff), lambda i, j: (i, j, 1)),
                  pl.BlockSpec((None, FFN_HALO, dff), halo(0)),
                  pl.BlockSpec((None, FFN_HALO, dff), halo(1)),
                  pl.BlockSpec((None, FFN_HALO, dff), lambda i, j: (i, 0, 0)),
                  pl.BlockSpec((None, FFN_HALO, dff), lambda i, j: (i, 0, 1)),
                  pl.BlockSpec((ktaps, two_dff), lambda i, j: (0, 0)),
                  pl.BlockSpec((dff, d), lambda i, j: (0, 0)),
                  pl.BlockSpec((None, tt, d), lambda i, j: (i, j, 0)),
                  pl.BlockSpec((1, d), lambda i, j: (0, 0))],
        out_specs=[pl.BlockSpec((None, tt, d), lambda i, j: (i, j, 0)),
                   pl.BlockSpec((None, tt, d), lambda i, j: (i, j, 0)),
                   pl.BlockSpec((None, FFN_HALO, two_dff), lambda i, j: (i, 0, 0))],
        out_shape=[jax.ShapeDtypeStruct((b, t, d), F32), jax.ShapeDtypeStruct((b, t, d), xn_dtype),
                   jax.ShapeDtypeStruct((b, FFN_HALO, two_dff), F32)],
        scratch_shapes=[pltpu.VMEM((FFN_HALO + tt, dff), F32), pltpu.VMEM((FFN_HALO + tt, dff), F32)],
        compiler_params=_cp(("parallel", "arbitrary")), name="ffn_tail",
    )(u3, u3, u3, u3, prev, prev, conv_w, w_down, x1_3, g_next.reshape(1, d))
    return x2, xn, st[:, FFN_HALO - (ktaps - 1):]


def _front_pad(x, rows):
    return jnp.pad(x, ((0, 0), (rows - x.shape[1], 0), (0, 0)))


def _layer(x3, xn2d, lw, layer, conv_a_prev, hg_prev, ffn_prev, nsa_fn, g_next, xn_dtype):
    b, t, d = x3.shape
    h2 = _matmul(xn2d, lw['w_in'], 1024, "in_proj")
    h3 = h2.reshape(b, t, N_PACK)
    gw = KVH * HD
    kv_rows = h3[:, :, OFF_KV:OFF_KV + 4 * gw].reshape(b, t, 4, KVH, HD)
    wrows = min(WINDOW, t)
    kv_win = h3[:, t - wrows:, OFF_KV + 4 * gw:OFF_KV + 6 * gw].reshape(b, wrows, 2, KVH, HD)

    za, conv_a_new = _conformer(h3, _front_pad(conv_a_prev, CONV_HALO), lw['conv_a_w'], lw['conv_a_b'],
                                lw['ln_a_g'], lw['ln_a_b'])
    zb = nsa_fn(h3)
    zc, hg_new = _hgrn(h3, hg_prev, lw['lb_logits'], lw['hg_norm_g'], layer)
    x1, xn2 = _merge(za.reshape(b * t, d), zb.reshape(b * t, d), zc.reshape(b * t, d), h2,
                     x3.reshape(b * t, d), lw['w_a_out'], lw['w_b_out'], lw['w_c_out'], lw['w_out'],
                     lw['norm2_g'])
    dff2 = lw['w_up'].shape[1]
    u2 = _matmul(xn2, lw['w_up'], dff2 // 2, "up_proj")
    x2, xn_next, ffn_new = _ffn_tail(u2.reshape(b, t, dff2), _front_pad(ffn_prev, FFN_HALO), lw['conv_f_w'],
                                     lw['w_down'], x1.reshape(b, t, d), g_next, xn_dtype)
    return x2, xn_next, kv_rows, kv_win, conv_a_new, hg_new, ffn_new


def kernel(x_prompt, x_sample, cache_nsa_kv, page_table, state_win_kv, state_conv_a, state_hgrn,
           state_ffn_conv, norm1_g, w_in, conv_a_w, conv_a_b, ln_a_g, ln_a_b, w_a_out, cmp_pe, cmp_w,
           w_b_out, hg_lb_logits, hg_norm_g, w_c_out, w_out, norm2_g, w_up, conv_f_w, w_down, final_g):
    depth = w_in.shape[0]
    bp, tp, d = x_prompt.shape
    bs, ts, _ = x_sample.shape
    assert d == 1024 and w_in.shape[2] == sum(IN_WIDTHS)
    cache_t = jnp.transpose(cache_nsa_kv, (0, 1, 3, 4, 5, 2))
    ka = conv_a_w.shape[1]
    kf = conv_f_w.shape[1]
    dff2 = w_up.shape[2]

    xp, xs = x_prompt, x_sample
    xnp = _rmsnorm(xp.reshape(bp * tp, d), norm1_g[0], BF16)
    xns = _rmsnorm(xs.reshape(bs * ts, d), norm1_g[0], BF16)
    outs = [[] for _ in range(10)]
    for l in range(depth):
        lw = {'w_in': _pack_w_in(w_in[l]), 'conv_a_w': conv_a_w[l], 'conv_a_b': conv_a_b[l],
              'ln_a_g': ln_a_g[l], 'ln_a_b': ln_a_b[l], 'w_a_out': w_a_out[l].astype(BF16),
              'w_b_out': w_b_out[l].astype(BF16), 'lb_logits': hg_lb_logits, 'hg_norm_g': hg_norm_g[l],
              'w_c_out': w_c_out[l].astype(BF16), 'w_out': w_out[l].astype(BF16), 'norm2_g': norm2_g[l],
              'w_up': w_up[l].astype(BF16), 'conv_f_w': conv_f_w[l], 'w_down': w_down[l].astype(BF16)}
        last = l == depth - 1
        g_next = final_g if last else norm1_g[l + 1]
        xn_dtype = F32 if last else BF16
        bd = _blockdiag_cmp_w(cmp_w[l])
        pe_t = _tile_pe(cmp_pe[l])
        nsa_p = functools.partial(_nsa_prompt, pe_t=pe_t, bd=bd)
        nsa_s = functools.partial(_nsa_sample, cache_t=cache_t, page_table=page_table, layer=l,
                                  win_prev=state_win_kv[l], pe=cmp_pe[l], bd=bd)
        xp, xnp, kv_p, win_p, ca_p, hg_p, ff_p = _layer(
            xp, xnp, lw, l, jnp.zeros((bp, ka - 1, W_A), F32), jnp.zeros((bp, H_C, DK_C, DV_C), F32),
            jnp.zeros((bp, kf - 1, dff2), F32), nsa_p, g_next, xn_dtype)
        xs, xns, kv_s, win_s, ca_s, hg_s, ff_s = _layer(
            xs, xns, lw, l, state_conv_a[l], state_hgrn[l], state_ffn_conv[l], nsa_s, g_next, xn_dtype)
        xnp = xnp.reshape(bp * tp, d)
        xns = xns.reshape(bs * ts, d)
        win_s_all = jnp.concatenate([state_win_kv[l], win_s], axis=1)
        for lst, v in zip(outs, (kv_p, kv_s, win_p, win_s_all[:, ts:], ca_p, ca_s, hg_p, hg_s, ff_p, ff_s)):
            lst.append(v)
    return (xnp.reshape(bp, tp, d), xns.reshape(bs, ts, d)) + tuple(jnp.stack(v) for v in outs)
```

```python
import functools
import math

import jax
import jax.numpy as jnp
import numpy as np
from jax import lax
from jax.experimental import pallas as pl
from jax.experimental.pallas import tpu as pltpu

F32 = jnp.float32
BF16 = jnp.bfloat16

H_B = 16
KVH = 4
HD = 64
HPG = H_B // KVH
L_CMP = 32
L_SEL = 64
N_SEL = 16
WINDOW = 512
H_C = 8
DK_C = 128
DV_C = 128
EPS = 1e-6
NEG = -1e30
BIG = 1e4
LB_TINY = 1e-30
SCALE = HD ** -0.5

LANE = 128
VMEM_LIMIT = 56 * 1024 * 1024

CONV_HALO = 32
FFN_HALO = 8
PAGES_PER_STEP = 16
NSB_PAD = 128
ROW_BLOCK = 256


def _cp(sem, vmem=VMEM_LIMIT):
    return pltpu.CompilerParams(dimension_semantics=sem, vmem_limit_bytes=vmem)


def _tile(n, pref):
    t = min(n, pref)
    while n % t:
        t -= 8
    assert t > 0
    return t


def _sigmoid(x):
    return 1.0 / (1.0 + jnp.exp(-x))


W_A = 1024
OFF_A = 0
OFF_GT = OFF_A + W_A
OFF_Q = OFF_GT + W_A
OFF_MG = OFF_Q + H_B * HD
OFF_CQ = OFF_MG + 3 * 1024
OFF_CF = OFF_CQ + H_C * DK_C
OFF_CI = OFF_CF + H_C * DK_C
OFF_CG = OFF_CI + H_C * DV_C
OFF_KV = OFF_CG + H_C * DV_C
OFF_BG = OFF_KV + 6 * KVH * HD
N_PACK = OFF_BG + KVH * LANE
IN_WIDTHS = (2 * W_A, H_B * HD, 6 * KVH * HD, 3 * H_B, H_C * DK_C, H_C * DK_C, H_C * DV_C, H_C * DV_C, 3 * 1024)


def _pack_w_in(w):
    d = w.shape[0]
    cuts = [int(c) for c in np.cumsum(IN_WIDTHS)[:-1]]
    a_in, b_q, b_kv, b_g, c_q, c_f, c_i, c_g, m_g = jnp.split(w, cuts, axis=1)
    bg = b_g.reshape(d, KVH, HPG * 3)
    bg = jnp.pad(bg, ((0, 0), (0, 0), (0, LANE - HPG * 3))).reshape(d, KVH * LANE)
    return jnp.concatenate([a_in, b_q, m_g, c_q, c_f, c_i, c_g, b_kv, bg], axis=1).astype(BF16)


def _rmsnorm_kernel(x_ref, g_ref, o_ref):
    x = x_ref[...]
    y = x * lax.rsqrt(jnp.mean(x * x, axis=-1, keepdims=True) + EPS)
    o_ref[...] = (y * g_ref[...]).astype(o_ref.dtype)


def _rmsnorm(x2d, g, out_dtype):
    m, d = x2d.shape
    tm = _tile(m, 512)
    return pl.pallas_call(
        _rmsnorm_kernel, grid=(m // tm,),
        in_specs=[pl.BlockSpec((tm, d), lambda i: (i, 0)), pl.BlockSpec((1, d), lambda i: (0, 0))],
        out_specs=pl.BlockSpec((tm, d), lambda i: (i, 0)),
        out_shape=jax.ShapeDtypeStruct((m, d), out_dtype),
        compiler_params=_cp(("parallel",)), name="rmsnorm")(x2d, g.reshape(1, d))


def _matmul_kernel(x_ref, w_ref, o_ref):
    o_ref[...] = jnp.dot(x_ref[...], w_ref[...], preferred_element_type=F32)


def _matmul(x, w, tn, name):
    m, k = x.shape
    n = w.shape[1]
    tm = _tile(m, 1024)
    assert n % tn == 0
    return pl.pallas_call(
        _matmul_kernel, grid=(n // tn, m // tm),
        in_specs=[pl.BlockSpec((tm, k), lambda j, i: (i, 0)), pl.BlockSpec((k, tn), lambda j, i: (0, j))],
        out_specs=pl.BlockSpec((tm, tn), lambda j, i: (i, j)),
        out_shape=jax.ShapeDtypeStruct((m, n), F32),
        compiler_params=_cp(("parallel", "parallel")), name=name)(x, w)


def _conformer_kernel(a_ref, g_ref, ah_ref, gh_ref, prev_ref, w_ref, b_ref, lg_ref, lb_ref,
                      z_ref, st_ref, full_ref, y_ref, *, ktaps, tt):
    t = pl.program_id(1)
    u_halo = ah_ref[...] * _sigmoid(gh_ref[...])
    full_ref[0:CONV_HALO, :] = jnp.where(t == 0, prev_ref[...], u_halo)
    full_ref[CONV_HALO:CONV_HALO + tt, :] = a_ref[...] * _sigmoid(g_ref[...])
    width = y_ref.shape[1]

    def lane_chunk(c, carry):
        cs = pl.ds(pl.multiple_of(c * LANE, LANE), LANE)
        acc = jnp.broadcast_to(b_ref[:, cs], (tt, LANE))
        for j in range(ktaps):
            acc = acc + w_ref[j:j + 1, cs] * full_ref[pl.ds(CONV_HALO - (ktaps - 1) + j, tt), cs]
        y_ref[:, cs] = acc
        return carry

    lax.fori_loop(0, width // LANE, lane_chunk, 0)
    y = y_ref[...]
    xc = y - jnp.mean(y, axis=-1, keepdims=True)
    yn = xc * lax.rsqrt(jnp.mean(xc * xc, axis=-1, keepdims=True) + EPS) * lg_ref[...] + lb_ref[...]
    z_ref[...] = (yn * _sigmoid(yn)).astype(z_ref.dtype)
    st_ref[...] = full_ref[tt:tt + CONV_HALO, :]


def _conformer(h3, prev, conv_w, conv_b, ln_g, ln_b):
    b, t, _ = h3.shape
    ktaps = conv_w.shape[0]
    assert ktaps - 1 <= CONV_HALO
    tt = _tile(t, 256)
    nt = t // tt
    if t >= CONV_HALO:
        assert tt % CONV_HALO == 0
        halo_src = h3
        ratio = tt // CONV_HALO
        halo_a = lambda i, j: (i, jnp.maximum(j * ratio - 1, 0), OFF_A // W_A)
        halo_g = lambda i, j: (i, jnp.maximum(j * ratio - 1, 0), OFF_GT // W_A)
    else:
        assert nt == 1
        halo_src = jnp.zeros((b, CONV_HALO, W_A), F32)
        halo_a = lambda i, j: (i, 0, 0)
        halo_g = halo_a
    row = lambda v: v.reshape(1, W_A)
    z, st = pl.pallas_call(
        functools.partial(_conformer_kernel, ktaps=ktaps, tt=tt), grid=(b, nt),
        in_specs=[pl.BlockSpec((None, tt, W_A), lambda i, j: (i, j, OFF_A // W_A)),
                  pl.BlockSpec((None, tt, W_A), lambda i, j: (i, j, OFF_GT // W_A)),
                  pl.BlockSpec((None, CONV_HALO, W_A), halo_a),
                  pl.BlockSpec((None, CONV_HALO, W_A), halo_g),
                  pl.BlockSpec((None, CONV_HALO, W_A), lambda i, j: (i, 0, 0)),
                  pl.BlockSpec((ktaps, W_A), lambda i, j: (0, 0)),
                  pl.BlockSpec((1, W_A), lambda i, j: (0, 0)),
                  pl.BlockSpec((1, W_A), lambda i, j: (0, 0)),
                  pl.BlockSpec((1, W_A), lambda i, j: (0, 0))],
        out_specs=[pl.BlockSpec((None, tt, W_A), lambda i, j: (i, j, 0)),
                   pl.BlockSpec((None, CONV_HALO, W_A), lambda i, j: (i, 0, 0))],
        out_shape=[jax.ShapeDtypeStruct((b, t, W_A), BF16), jax.ShapeDtypeStruct((b, CONV_HALO, W_A), F32)],
        scratch_shapes=[pltpu.VMEM((CONV_HALO + tt, W_A), F32), pltpu.VMEM((tt, W_A), F32)],
        compiler_params=_cp(("parallel", "arbitrary")), name="conformer",
    )(h3, h3, halo_src, halo_src, prev, conv_w, row(conv_b), row(ln_g), row(ln_b))
    return z, st[:, CONV_HALO - (ktaps - 1):]


def _blockdiag_cmp_w(cw):
    eye = jnp.eye(KVH, dtype=cw.dtype)
    bd = jnp.einsum('klde,gh->klgdhe', cw, eye)
    return bd.reshape(2, L_CMP * KVH * HD, KVH * HD).astype(BF16)


def _tile_pe(pe):
    return jnp.broadcast_to(pe[:, :, None, :], (2, L_CMP, KVH, HD)).reshape(2, 1, L_CMP * KVH * HD)


def _compress_kernel(x_ref, pe_ref, w_ref, o_ref):
    x = (x_ref[...] + pe_ref[...]).astype(BF16)
    o_ref[...] = jnp.dot(x, w_ref[...], preferred_element_type=F32)


def _compress(x, pe_t, bd):
    _, r, kdim = x.shape
    n = KVH * HD
    tr = _tile(r, 256)
    return pl.pallas_call(
        _compress_kernel, grid=(2, r // tr),
        in_specs=[pl.BlockSpec((None, tr, kdim), lambda s, i: (s, i, 0)),
                  pl.BlockSpec((None, 1, kdim), lambda s, i: (s, 0, 0)),
                  pl.BlockSpec((None, kdim, n), lambda s, i: (s, 0, 0))],
        out_specs=pl.BlockSpec((None, tr, n), lambda s, i: (s, i, 0)),
        out_shape=jax.ShapeDtypeStruct((2, r, n), F32),
        compiler_params=_cp(("parallel", "parallel")), name="nsa_compress")(x, pe_t, bd)


def _even_odd(x, axis):
    n = x.shape[axis]
    y = x.reshape(x.shape[:axis] + (n // 2, 2) + x.shape[axis + 1:])
    return jnp.swapaxes(y, axis, axis + 1).reshape(x.shape)


def _rank_select_rows(score, k):
    n = score.shape[0]
    jrow = lax.broadcasted_iota(jnp.int32, score.shape, 0)
    cnt = jnp.zeros(score.shape, F32)
    for i in range(n):
        si = score[i:i + 1, :]
        ge = jnp.where(si >= score, 1.0, 0.0)
        gt = jnp.where(si > score, 1.0, 0.0)
        cnt = cnt + jnp.where(jrow > i, ge, gt)
    return jnp.where(cnt < k, 1.0, 0.0)


def _topk_lanes(score, k):
    n = score.shape[-1]
    cidx = lax.broadcasted_iota(jnp.int32, score.shape, score.ndim - 1).astype(F32)
    sel = jnp.zeros(score.shape, F32)
    s = score
    for _ in range(k):
        m = jnp.max(s, axis=-1, keepdims=True)
        imin = jnp.min(jnp.where(s == m, cidx, float(n)), axis=-1, keepdims=True)
        hit = cidx == imin
        sel = jnp.where(hit, 1.0, sel)
        s = jnp.where(hit, -jnp.inf, s)
    return sel


_NT = (((1,), (1,)), ((), ()))
_TN = (((0,), (0,)), ((), ()))


def _ones_lanes(v):
    return jnp.concatenate([v, jnp.ones(v.shape[:-1] + (LANE - v.shape[-1],), v.dtype)], axis=-1)


def _flash_update(m_prev, acc_prev, s, v_aug, nt=False):
    m_new = jnp.maximum(m_prev, jnp.max(s, axis=1, keepdims=True))
    alpha = jnp.exp(m_prev - m_new)
    p = jnp.exp(s - m_new[:, 0:1]).astype(BF16)
    if nt:
        pv = lax.dot_general(p, v_aug, _NT, preferred_element_type=F32)
    else:
        pv = jnp.dot(p, v_aug, preferred_element_type=F32)
    return m_new, acc_prev * alpha + pv


def _flash_step(j, s, v_aug, m_sc, acc_sc, nt=False):
    m_sc[j], acc_sc[j] = _flash_update(m_sc[j], acc_sc[j], s, v_aug, nt)


def _flash_reset(m_sc, acc_sc):
    m_sc[...] = jnp.full(m_sc.shape, NEG, F32)
    acc_sc[...] = jnp.zeros(acc_sc.shape, F32)


def _flash_out(acc):
    return acc[:, 0:HD] / acc[:, HD:2 * HD]


def _nsa_prompt_kernel(q_ref, bg_ref, kc_ref, vc_ref, ks_ref, vs_ref, kw_ref, vw_ref, ex_ref, o_ref,
                       q_sc, selm_sc, m_sc, acc_sc, win_sc, bias_sc, *, tq, tk, wk, nb, nsb):
    t0 = pl.program_id(2) * tq
    q = (q_ref[...] * SCALE).astype(BF16)
    for j in range(HPG):
        q_sc[j] = q[:, j * HD:(j + 1) * HD]
    kc = kc_ref[...]
    vc = vc_ref[...]

    half = nb // 2
    row = lax.broadcasted_iota(jnp.int32, (nb, tq), 0)
    tpos = t0 + lax.broadcasted_iota(jnp.int32, (nb, tq), 1)
    blk = jnp.where(row < half, 2 * row, 2 * (row - half) + 1)
    cvalid = (blk + 1) * L_CMP - 1 <= tpos
    sts = [lax.dot_general(kc, q_sc[j], _NT, preferred_element_type=F32) for j in range(HPG)]
    pcs = []
    for j in range(HPG):
        sm = jnp.where(cvalid, sts[j], NEG)
        e = jnp.exp(sm - jnp.max(sm, axis=0, keepdims=True))
        pcs.append(jnp.where(cvalid, e / jnp.sum(e, axis=0, keepdims=True), 0.0))
    o_cmp = [lax.dot_general(pcs[j].astype(BF16), vc, _TN, preferred_element_type=F32) for j in range(HPG)]
    pg = pcs[0]
    for j in range(1, HPG):
        pg = pg + pcs[j]

    ps = pg[:half] + pg[half:]
    jrow = lax.broadcasted_iota(jnp.int32, (nsb, tq), 0)
    tp = t0 + lax.broadcasted_iota(jnp.int32, (nsb, tq), 1)
    cur = tp // L_SEL
    forced_ps = jnp.where(jrow == 0, BIG, jnp.where(jrow == cur, BIG, jnp.where(jrow == cur - 1, BIG, ps)))
    score = jnp.where(jrow * L_SEL <= tp, forced_ps, -2.0 * BIG)
    sel = _rank_select_rows(score, min(N_SEL, nsb))
    sel = jnp.concatenate([sel, jnp.zeros((NSB_PAD - nsb, tq), F32)], axis=0)
    selm_sc[...] = jnp.transpose(sel).astype(BF16)

    _flash_reset(m_sc, acc_sc)
    nrb = tq // ROW_BLOCK

    def sel_chunk(c, carry):
        k0 = pl.multiple_of(c * tk, tk)
        chosen = jnp.dot(selm_sc[...], ex_ref[:, pl.ds(k0, tk)], preferred_element_type=F32)
        qpos = t0 + lax.broadcasted_iota(jnp.int32, (tq, tk), 0)
        kpos = k0 + lax.broadcasted_iota(jnp.int32, (tq, tk), 1)
        bias_sc[:, 0:tk] = jnp.where(kpos <= qpos, (chosen - 1.0) * (-NEG), NEG)

        def rows(rb, carry2):
            rs = pl.ds(pl.multiple_of(rb * ROW_BLOCK, ROW_BLOCK), ROW_BLOCK)
            ks = ks_ref[pl.ds(k0, tk), :]
            vs = vs_ref[pl.ds(k0, tk), :]
            bias = bias_sc[rs, 0:tk]
            heads = range(HPG)
            ss = [(lax.dot_general(q_sc[j, rs, :], ks, _NT, preferred_element_type=F32) + bias).astype(BF16)
                  for j in heads]
            m_prev = [m_sc[j, rs, :] for j in heads]
            m_new = [jnp.maximum(m_prev[j], jnp.max(ss[j], axis=1, keepdims=True).astype(F32)) for j in heads]
            ps = [jnp.exp(ss[j] - m_new[j][:, 0:1].astype(BF16)) for j in heads]
            pvs = [jnp.dot(ps[j], vs, preferred_element_type=F32) for j in heads]
            for j in heads:
                acc_sc[j, rs, :] = acc_sc[j, rs, :] * jnp.exp(m_prev[j] - m_new[j]) + pvs[j]
                m_sc[j, rs, :] = m_new[j]
            return carry2

        lax.fori_loop(0, nrb, rows, 0)
        return carry

    lax.fori_loop(0, (t0 + tq + tk - 1) // tk, sel_chunk, 0)

    w0 = pl.multiple_of(jnp.maximum(t0 - WINDOW, 0), LANE)
    d = (t0 + lax.broadcasted_iota(jnp.int32, (tq, wk), 0)) - (w0 + lax.broadcasted_iota(jnp.int32, (tq, wk), 1))
    bias_sc[:, 0:wk] = jnp.where(d >= 0, jnp.where(d < WINDOW, 0.0, NEG), NEG)

    def win_rows(rb, carry):
        rs = pl.ds(pl.multiple_of(rb * ROW_BLOCK, ROW_BLOCK), ROW_BLOCK)
        kw = kw_ref[pl.ds(w0, wk), :]
        vw = vw_ref[pl.ds(w0, wk), :]
        bias = bias_sc[rs, 0:wk]
        heads = range(HPG)
        ss = [(lax.dot_general(q_sc[j, rs, :], kw, _NT, preferred_element_type=F32) + bias).astype(BF16)
              for j in heads]
        ps = [jnp.exp(ss[j] - jnp.max(ss[j], axis=1, keepdims=True)) for j in heads]
        pvs = [jnp.dot(ps[j], vw, preferred_element_type=F32) for j in heads]
        for j in heads:
            win_sc[j, rs, :] = pvs[j]
        return carry

    lax.fori_loop(0, nrb, win_rows, 0)

    sg = _sigmoid(bg_ref[...])
    outs = []
    for j in range(HPG):
        outs.append(sg[:, 3 * j:3 * j + 1] * o_cmp[j] + sg[:, 3 * j + 1:3 * j + 2] * _flash_out(acc_sc[j])
                    + sg[:, 3 * j + 2:3 * j + 3] * _flash_out(win_sc[j]))
    o_ref[...] = jnp.concatenate(outs, axis=1).astype(o_ref.dtype)


def _nsa_prompt(h3, pe_t, bd):
    b, t, _ = h3.shape
    tq, tk = 256, 512
    wk = WINDOW + tq
    assert t % tk == 0 and t % L_SEL == 0 and t >= wk and WINDOW % LANE == 0 and tq % ROW_BLOCK == 0
    nb = t // L_CMP
    nsb = t // L_SEL
    assert nsb <= NSB_PAD and nb == 2 * nsb
    kv = h3[:, :, OFF_KV:OFF_KV + 6 * KVH * HD]
    xc = jnp.stack([kv[:, :, 0:KVH * HD].reshape(b * nb, L_CMP * KVH * HD),
                    kv[:, :, KVH * HD:2 * KVH * HD].reshape(b * nb, L_CMP * KVH * HD)])
    cmp = _compress(xc, pe_t, bd).reshape(2, b, nb, KVH, HD)
    cmp = jnp.transpose(_even_odd(cmp, 2), (0, 1, 3, 2, 4)).astype(BF16)
    kvh = jnp.transpose(kv[:, :, 2 * KVH * HD:].reshape(b, t, 4, KVH, HD), (2, 0, 3, 1, 4)).astype(BF16)
    expand = jnp.asarray(np.arange(NSB_PAD)[:, None] == np.arange(t)[None, :] // L_SEL, dtype=BF16)
    qcol = OFF_Q // (HPG * HD)
    bgcol = OFF_BG // LANE
    head = lambda w: pl.BlockSpec((None, None, t, w), lambda i, g, n: (i, g, 0, 0))
    return pl.pallas_call(
        functools.partial(_nsa_prompt_kernel, tq=tq, tk=tk, wk=wk, nb=nb, nsb=nsb), grid=(b, KVH, t // tq),
        in_specs=[pl.BlockSpec((None, tq, HPG * HD), lambda i, g, n: (i, n, qcol + g)),
                  pl.BlockSpec((None, tq, LANE), lambda i, g, n: (i, n, bgcol + g)),
                  pl.BlockSpec((None, None, nb, HD), lambda i, g, n: (i, g, 0, 0)),
                  pl.BlockSpec((None, None, nb, HD), lambda i, g, n: (i, g, 0, 0)),
                  head(HD), head(LANE), head(HD), head(LANE),
                  pl.BlockSpec((NSB_PAD, t), lambda i, g, n: (0, 0))],
        out_specs=pl.BlockSpec((None, tq, HPG * HD), lambda i, g, n: (i, n, g)),
        out_shape=jax.ShapeDtypeStruct((b, t, H_B * HD), BF16),
        scratch_shapes=[pltpu.VMEM((HPG, tq, HD), BF16), pltpu.VMEM((tq, NSB_PAD), BF16),
                        pltpu.VMEM((HPG, tq, LANE), F32), pltpu.VMEM((HPG, tq, LANE), F32),
                        pltpu.VMEM((HPG, tq, LANE), F32), pltpu.VMEM((tq, max(tk, wk)), F32)],
        compiler_params=_cp(("parallel", "parallel", "arbitrary")), name="nsa_prompt",
    )(h3, h3, cmp[0], cmp[1], kvh[0], _ones_lanes(kvh[1]), kvh[2], _ones_lanes(kvh[3]), expand)


def _cmp_perm(page):
    per = page // L_CMP
    m = np.zeros((2 * page, 2 * page), np.float32)
    for l in range(L_CMP):
        for p in range(2):
            for n in range(per):
                m[l * 2 * per + p * per + n, p * page + n * L_CMP + l] = 1.0
    return jnp.asarray(m, dtype=BF16)


def _cmp_past_kernel(pt_ref, *refs, n_steps, per):
    pages = refs[:PAGES_PER_STEP]
    pe_ref, perm_ref, w_ref, o_ref, x_sc = refs[PAGES_PER_STEP:]
    pg = pl.program_id(1)
    gd = KVH * HD
    page = pages[0].shape[-1]
    rows = PAGES_PER_STEP * per
    grp = 2 * per
    for s in range(2):
        moved = []
        for pr in range(PAGES_PER_STEP // 2):
            xt = jnp.concatenate([pages[2 * pr][s].reshape(gd, page), pages[2 * pr + 1][s].reshape(gd, page)], axis=1)
            xt = (xt + pe_ref[s]).astype(BF16)
            moved.append(lax.dot_general(perm_ref[...], xt, _NT, preferred_element_type=F32))
        for l in range(L_CMP):
            slab = jnp.concatenate([m[l * grp:(l + 1) * grp] for m in moved], axis=0)
            x_sc[s, l, pl.ds(pl.multiple_of(pg * rows, rows), rows), :] = slab.astype(BF16)

    @pl.when(pg == n_steps - 1)
    def _():
        o_ref[...] = jnp.zeros(o_ref.shape, F32)

        def body(l, carry):
            for s in range(2):
                o_ref[s] += jnp.dot(x_sc[s, l], w_ref[s, l], preferred_element_type=F32)
            return carry

        lax.fori_loop(0, L_CMP, body, 0)


def _cmp_past(cache_t, page_table, layer, pe, bd):
    b, n_pages = page_table.shape
    page = cache_t.shape[-1]
    per = page // L_CMP
    gd = KVH * HD
    assert n_pages % PAGES_PER_STEP == 0 and PAGES_PER_STEP % 2 == 0 and 2 * per == 8
    n_steps = n_pages // PAGES_PER_STEP
    nbp = n_pages * per
    pe_cols = jnp.tile(jnp.swapaxes(pe, 1, 2), (1, KVH, 2 * per))
    specs = []
    for slot in range(PAGES_PER_STEP):
        def index_map(i, pg, pt, slot=slot):
            return (layer, pt[i, pg * PAGES_PER_STEP + slot], 0, 0, 0, 0)
        specs.append(pl.BlockSpec((None, None, 2, KVH, HD, page), index_map))
    grid_spec = pltpu.PrefetchScalarGridSpec(
        num_scalar_prefetch=1, grid=(b, n_steps),
        in_specs=specs + [pl.BlockSpec((2, gd, 2 * page), lambda i, pg, pt: (0, 0, 0)),
                          pl.BlockSpec((2 * page, 2 * page), lambda i, pg, pt: (0, 0)),
                          pl.BlockSpec((2, L_CMP, gd, gd), lambda i, pg, pt: (0, 0, 0, 0))],
        out_specs=pl.BlockSpec((None, 2, nbp, gd), lambda i, pg, pt: (i, 0, 0, 0)),
        scratch_shapes=[pltpu.VMEM((2, L_CMP, nbp, gd), BF16)])
    return pl.pallas_call(
        functools.partial(_cmp_past_kernel, n_steps=n_steps, per=per), grid_spec=grid_spec,
        out_shape=jax.ShapeDtypeStruct((b, 2, nbp, gd), F32),
        compiler_params=_cp(("parallel", "arbitrary")), name="nsa_cmp_past",
    )(page_table, *([cache_t] * PAGES_PER_STEP), pe_cols, _cmp_perm(page), bd.reshape(2, L_CMP, gd, gd))


def _stack_heads(x, g):
    return jnp.concatenate([x[:, (g * HPG + j) * HD:(g * HPG + j + 1) * HD] for j in range(HPG)], axis=0)


def _sample_cmp_win_kernel(q_ref, bg_ref, kc_ref, vc_ref, wp_ref, wn_ref, ocw_ref, sel_ref,
                           *, t, past, nbp, nsb, sel_pad):
    q = (q_ref[...] * SCALE).astype(BF16)
    sg = _sigmoid(bg_ref[...])
    rows = HPG * t
    half = nbp // 2
    wb = wp_ref.shape[0]
    gw = KVH * HD

    col = lax.broadcasted_iota(jnp.int32, (rows, nbp), 1)
    tq = lax.broadcasted_iota(jnp.int32, (rows, nbp), 0) % t
    blk = jnp.where(col < half, 2 * col, 2 * (col - half) + 1)
    cvalid = (blk + 1) * L_CMP - 1 <= past + tq

    wi = lax.broadcasted_iota(jnp.int32, (rows, wb), 1)
    wt = lax.broadcasted_iota(jnp.int32, (rows, wb), 0) % t
    d_prev = wt + wb - wi
    ni = lax.broadcasted_iota(jnp.int32, (rows, t), 1)
    nt = lax.broadcasted_iota(jnp.int32, (rows, t), 0) % t
    d_new = nt - ni

    ps_all = []
    outs = [None] * H_B
    for g in range(KVH):
        qg = _stack_heads(q, g)
        kc = kc_ref[:, g * HD:(g + 1) * HD].astype(BF16)
        vc = vc_ref[:, g * HD:(g + 1) * HD].astype(BF16)
        s = lax.dot_general(qg, kc, _NT, preferred_element_type=F32)
        sm = jnp.where(cvalid, s, NEG)
        e = jnp.exp(sm - jnp.max(sm, axis=1, keepdims=True))
        p = jnp.where(cvalid, e / jnp.sum(e, axis=1, keepdims=True), 0.0)
        o_cmp = jnp.dot(p.astype(BF16), vc, preferred_element_type=F32)
        pgrp = p[0:t]
        for j in range(1, HPG):
            pgrp = pgrp + p[j * t:(j + 1) * t]
        ps_all.append(pgrp[:, :half] + pgrp[:, half:])

        kp = wp_ref[:, g * HD:(g + 1) * HD].astype(BF16)
        vp = wp_ref[:, gw + g * HD:gw + (g + 1) * HD].astype(BF16)
        kn = wn_ref[:, g * HD:(g + 1) * HD].astype(BF16)
        vn = wn_ref[:, gw + g * HD:gw + (g + 1) * HD].astype(BF16)
        s1 = lax.dot_general(qg, kp, _NT, preferred_element_type=F32)
        s1 = jnp.where(d_prev >= 0, jnp.where(d_prev < WINDOW, s1, NEG), NEG)
        s2 = lax.dot_general(qg, kn, _NT, preferred_element_type=F32)
        s2 = jnp.where(d_new >= 0, jnp.where(d_new < WINDOW, s2, NEG), NEG)
        mx = jnp.maximum(jnp.max(s1, axis=1, keepdims=True), jnp.max(s2, axis=1, keepdims=True))
        e1 = jnp.exp(s1 - mx)
        e2 = jnp.exp(s2 - mx)
        den = jnp.sum(e1, axis=1, keepdims=True) + jnp.sum(e2, axis=1, keepdims=True)
        o_win = (jnp.dot(e1.astype(BF16), vp, preferred_element_type=F32)
                 + jnp.dot(e2.astype(BF16), vn, preferred_element_type=F32)) / den
        for j in range(HPG):
            c = g * LANE + 3 * j
            outs[g * HPG + j] = (sg[:, c:c + 1] * o_cmp[j * t:(j + 1) * t]
                                 + sg[:, c + 2:c + 3] * o_win[j * t:(j + 1) * t])
    ocw_ref[...] = jnp.concatenate(outs, axis=1)

    ps = jnp.concatenate(ps_all, axis=0)
    ps = jnp.concatenate([ps, jnp.zeros((KVH * t, sel_pad - half), F32)], axis=1)
    jb = lax.broadcasted_iota(jnp.int32, (KVH * t, sel_pad), 1)
    qp = past + lax.broadcasted_iota(jnp.int32, (KVH * t, sel_pad), 0) % t
    cur = qp // L_SEL
    forced_ps = jnp.where(jb == 0, BIG, jnp.where(jb == cur, BIG, jnp.where(jb == cur - 1, BIG, ps)))
    score = jnp.where(jb * L_SEL <= qp, forced_ps, -2.0 * BIG)
    score = jnp.where(jb < nsb, score, -jnp.inf)
    sel_ref[...] = _topk_lanes(score, min(N_SEL, nsb))


def _sample_sel_kernel(pt_ref, *refs, t, n_steps):
    pages = refs[:PAGES_PER_STEP]
    q_ref, bg_ref, selp_ref, seln_ref, ex_ref, kvn_ref, ocw_ref, o_ref, q_sc, m_sc, acc_sc = refs[PAGES_PER_STEP:]
    pg = pl.program_id(1)
    rows = HPG * t
    gw = KVH * HD
    keys = PAGES_PER_STEP * pages[0].shape[-1]

    @pl.when(pg == 0)
    def _():
        q = (q_ref[...] * SCALE).astype(BF16)
        for g in range(KVH):
            q_sc[g] = _stack_heads(q, g)
        _flash_reset(m_sc, acc_sc)

    def per_head_rows(x):
        return jnp.concatenate([x[g * t:(g + 1) * t] for g in range(KVH) for _ in range(HPG)], axis=0)

    chosen = jnp.dot(per_head_rows(selp_ref[...]).astype(BF16), ex_ref[...], preferred_element_type=F32)
    ones = jnp.ones((LANE - HD, keys), BF16)
    bias = (chosen - 1.0) * (-NEG)
    groups = range(KVH)
    ss = []
    for g in groups:
        kt = jnp.concatenate([p[0, g] for p in pages], axis=1).astype(BF16)
        ss.append(jnp.dot(q_sc[g], kt, preferred_element_type=F32))
    s = jnp.concatenate(ss, axis=0) + bias
    m_prev = m_sc[...]
    m_new = jnp.maximum(m_prev, jnp.max(s, axis=1, keepdims=True))
    p = jnp.exp(s - m_new[:, 0:1]).astype(BF16)
    pvs = []
    for g in groups:
        vt = jnp.concatenate([pp[1, g] for pp in pages], axis=1).astype(BF16)
        pvs.append(lax.dot_general(p[g * rows:(g + 1) * rows], jnp.concatenate([vt, ones], axis=0), _NT,
                                   preferred_element_type=F32))
    acc_sc[...] = acc_sc[...] * jnp.exp(m_prev - m_new) + jnp.concatenate(pvs, axis=0)
    m_sc[...] = m_new

    @pl.when(pg == n_steps - 1)
    def _():
        sg = _sigmoid(bg_ref[...])
        ni = lax.broadcasted_iota(jnp.int32, (rows, t), 1)
        nt = lax.broadcasted_iota(jnp.int32, (rows, t), 0) % t
        new_chosen = per_head_rows(seln_ref[...])[:, 0:1] > 0.5
        outs = []
        for g in range(KVH):
            rs = slice(g * rows, (g + 1) * rows)
            kn = kvn_ref[:, g * HD:(g + 1) * HD].astype(BF16)
            vn = _ones_lanes(kvn_ref[:, gw + g * HD:gw + (g + 1) * HD].astype(BF16))
            s = lax.dot_general(q_sc[g], kn, _NT, preferred_element_type=F32)
            s = jnp.where(ni <= nt, jnp.where(new_chosen[rs], s, NEG), NEG)
            _flash_step(rs, s, vn, m_sc, acc_sc)
            o_sel = _flash_out(acc_sc[rs])
            for j in range(HPG):
                c = g * LANE + 3 * j + 1
                outs.append(sg[:, c:c + 1] * o_sel[j * t:(j + 1) * t])
        o_ref[...] = (ocw_ref[...] + jnp.concatenate(outs, axis=1)).astype(o_ref.dtype)


def _nsa_sample(h3, cache_t, page_table, layer, win_prev, pe, bd):
    b, t, _ = h3.shape
    n_pages = page_table.shape[1]
    page = cache_t.shape[-1]
    past = n_pages * page
    assert t < L_CMP and t <= L_SEL and past % L_SEL == 0 and t % 8 == 0
    nbp = past // L_CMP
    n_past_blk = past // L_SEL
    nsb = -(-(past + t) // L_SEL)
    assert nsb == n_past_blk + 1
    sel_pad = -(-nsb // LANE) * LANE
    gw = KVH * HD
    cmp = _even_odd(_cmp_past(cache_t, page_table, layer, pe, bd), 2)
    wb = win_prev.shape[1]
    wp = win_prev.reshape(b, wb, 2 * gw)
    ocw, sel = pl.pallas_call(
        functools.partial(_sample_cmp_win_kernel, t=t, past=past, nbp=nbp, nsb=nsb, sel_pad=sel_pad),
        grid=(b,),
        in_specs=[pl.BlockSpec((None, t, H_B * HD), lambda i: (i, 0, OFF_Q // (H_B * HD))),
                  pl.BlockSpec((None, t, KVH * LANE), lambda i: (i, 0, OFF_BG // (KVH * LANE))),
                  pl.BlockSpec((None, None, nbp, gw), lambda i: (i, 0, 0, 0)),
                  pl.BlockSpec((None, None, nbp, gw), lambda i: (i, 1, 0, 0)),
                  pl.BlockSpec((None, wb, 2 * gw), lambda i: (i, 0, 0)),
                  pl.BlockSpec((None, t, 2 * gw), lambda i: (i, 0, (OFF_KV + 4 * gw) // (2 * gw)))],
        out_specs=[pl.BlockSpec((None, t, H_B * HD), lambda i: (i, 0, 0)),
                   pl.BlockSpec((None, KVH * t, sel_pad), lambda i: (i, 0, 0))],
        out_shape=[jax.ShapeDtypeStruct((b, t, H_B * HD), F32),
                   jax.ShapeDtypeStruct((b, KVH * t, sel_pad), F32)],
        compiler_params=_cp(("parallel",)), name="nsa_sample_cmp_win",
    )(h3, h3, cmp, cmp, wp, h3)

    n_steps = n_pages // PAGES_PER_STEP
    keys = PAGES_PER_STEP * page
    bps = keys // L_SEL
    assert keys % L_SEL == 0 and bps <= LANE
    lane_pad = lambda x: jnp.pad(x, [(0, 0)] * (x.ndim - 1) + [(0, LANE - x.shape[-1])])
    sel_past = lane_pad(jnp.swapaxes(sel[:, :, :n_past_blk].reshape(b, KVH * t, n_steps, bps), 1, 2))
    sel_new = lane_pad(sel[:, :, n_past_blk:n_past_blk + 1])
    expand = jnp.asarray(np.arange(LANE)[:, None] == np.arange(keys)[None, :] // L_SEL, dtype=BF16)
    specs = []
    for slot in range(PAGES_PER_STEP):
        def index_map(i, pg, pt, slot=slot):
            return (layer, pt[i, pg * PAGES_PER_STEP + slot], 1, 0, 0, 0)
        specs.append(pl.BlockSpec((None, None, 2, KVH, HD, page), index_map))
    nrow = KVH * HPG * t
    grid_spec = pltpu.PrefetchScalarGridSpec(
        num_scalar_prefetch=1, grid=(b, n_steps),
        in_specs=specs + [
            pl.BlockSpec((None, t, H_B * HD), lambda i, pg, pt: (i, 0, OFF_Q // (H_B * HD))),
            pl.BlockSpec((None, t, KVH * LANE), lambda i, pg, pt: (i, 0, OFF_BG // (KVH * LANE))),
            pl.BlockSpec((None, None, KVH * t, LANE), lambda i, pg, pt: (i, pg, 0, 0)),
            pl.BlockSpec((None, KVH * t, LANE), lambda i, pg, pt: (i, 0, 0)),
            pl.BlockSpec((LANE, keys), lambda i, pg, pt: (0, 0)),
            pl.BlockSpec((None, t, 2 * gw), lambda i, pg, pt: (i, 0, (OFF_KV + 2 * gw) // (2 * gw))),
            pl.BlockSpec((None, t, H_B * HD), lambda i, pg, pt: (i, 0, 0))],
        out_specs=pl.BlockSpec((None, t, H_B * HD), lambda i, pg, pt: (i, 0, 0)),
        scratch_shapes=[pltpu.VMEM((KVH, HPG * t, HD), BF16), pltpu.VMEM((nrow, LANE), F32),
                        pltpu.VMEM((nrow, LANE), F32)])
    return pl.pallas_call(
        functools.partial(_sample_sel_kernel, t=t, n_steps=n_steps),
        grid_spec=grid_spec, out_shape=jax.ShapeDtypeStruct((b, t, H_B * HD), BF16),
        compiler_params=_cp(("parallel", "arbitrary")), name="nsa_sample_sel",
    )(page_table, *([cache_t] * PAGES_PER_STEP), h3, h3, sel_past, sel_new, expand, h3, ocw)


def _hgrn_lmats(c):
    nlev = int(math.log2(c))
    assert 1 << nlev == c
    r = np.arange(c)[:, None]
    i = np.arange(c)[None, :]
    mats = [i <= r, i > r]
    pairs = []
    for lev in range(nlev):
        blk = c >> lev
        mid = (r // blk) * blk + blk // 2
        mats.append(np.where(r >= mid, (i >= mid) & (i <= r), (i > r) & (i < mid)))
        pairs.append((r // blk == i // blk) & (r % blk >= blk // 2) & (i % blk < blk // 2))
    return (jnp.asarray(np.concatenate(mats, axis=0).astype(np.float32), dtype=BF16),
            jnp.asarray(np.stack(pairs).astype(np.float32)))


HGRN_HEADS_PER_STEP = 4


def _hgrn_kernel(cq_ref, cf_ref, ci_ref, cg_ref, lbl_ref, ng_ref, s0_ref, lm_ref, pm_ref, z_ref, sn_ref, st_sc,
                 *, c, nch, layer, nlev):
    tstep = pl.program_id(2)
    nh = HGRN_HEADS_PER_STEP

    @pl.when(tstep == 0)
    def _():
        for hh in range(nh):
            st_sc[hh] = jnp.transpose(s0_ref[hh])

    lg = lbl_ref[...]
    e = jnp.exp(lg - jnp.max(lg, axis=0, keepdims=True))
    p = e / jnp.sum(e, axis=0, keepdims=True)
    cs = p[0:1]
    for i in range(1, layer + 1):
        cs = cs + p[i:i + 1]
    lb_all = cs - p[layer:layer + 1]
    lm = lm_ref[...]
    cp = max(c, LANE)

    def chunk(ci, carry):
        rs = pl.ds(pl.multiple_of(ci * c, c), c)
        heads = range(nh)
        hsl = [slice(hh * DK_C, (hh + 1) * DK_C) for hh in heads]
        fp = [cf_ref[rs, hs] for hs in hsl]
        cq = [cq_ref[rs, hs] for hs in hsl]
        v = [ci_ref[rs, hs] for hs in hsl]
        kk, qq, l2 = [], [], []
        for hh in heads:
            lb = lb_all[:, hsl[hh]]
            log_lb = jnp.log(jnp.maximum(lb, LB_TINY))
            log_sig = jnp.minimum(fp[hh], 0.0) - jnp.log(1.0 + jnp.exp(-jnp.abs(fp[hh])))
            x2 = jnp.log1p(-lb) + log_sig
            logf = jnp.where(lb > 0, jnp.maximum(log_lb, x2) + jnp.log(1.0 + jnp.exp(-jnp.abs(log_lb - x2))),
                             log_sig)
            kk.append((1.0 - lb) * _sigmoid(-fp[hh]))
            qq.append(cq[hh] * _sigmoid(cq[hh]))
            hi = logf.astype(BF16)
            l2.append(jnp.concatenate([hi, (logf - hi.astype(F32)).astype(BF16)], axis=1))
        d2 = [jnp.dot(lm, l2[hh], preferred_element_type=F32) for hh in heads]
        dd = [d[:, :DK_C] + d[:, DK_C:] for d in d2]
        st = [st_sc[hh] for hh in heads]
        o1 = [lax.dot_general((qq[hh] * jnp.exp(dd[hh][0:c])).astype(BF16), st[hh].astype(BF16), _NT,
                              preferred_element_type=F32) for hh in heads]
        al = []
        for lev in range(nlev):
            for hh in heads:
                ee = jnp.exp(dd[hh][(2 + lev) * c:(3 + lev) * c])
                al.append(lax.dot_general((qq[hh] * ee).astype(BF16), (kk[hh] * ee).astype(BF16), _NT,
                                          preferred_element_type=F32))
        amat = []
        for hh in heads:
            acc = al[hh] * pm_ref[0]
            for lev in range(1, nlev):
                acc = acc + al[lev * nh + hh] * pm_ref[lev]
            amat.append(acc.astype(BF16))
        o2 = [jnp.dot(amat[hh], v[hh].astype(BF16), preferred_element_type=F32) for hh in heads]
        upd = []
        for hh in heads:
            kd = kk[hh] * jnp.exp(dd[hh][c:2 * c])
            vp = v[hh]
            if cp > c:
                zpad = jnp.zeros((cp - c, DK_C), F32)
                kd = jnp.concatenate([kd, zpad], axis=0)
                vp = jnp.concatenate([vp, zpad], axis=0)
            upd.append(jnp.dot(jnp.transpose(vp).astype(BF16), kd.astype(BF16), preferred_element_type=F32))
        for hh in heads:
            st_sc[hh] = st[hh] * jnp.exp(dd[hh][c - 1:c, :]) + upd[hh]
            o = o1[hh] + o2[hh] + jnp.sum(qq[hh] * kk[hh], axis=1, keepdims=True) * v[hh]
            y = o * lax.rsqrt(jnp.mean(o * o, axis=1, keepdims=True) + EPS) * ng_ref[:, hsl[hh]]
            cg = cg_ref[rs, hsl[hh]]
            z_ref[rs, hsl[hh]] = (y * (cg * _sigmoid(cg))).astype(z_ref.dtype)
        return carry

    lax.fori_loop(0, nch, chunk, 0)

    @pl.when(tstep == pl.num_programs(2) - 1)
    def _():
        for hh in range(nh):
            sn_ref[hh] = jnp.transpose(st_sc[hh])


def _hgrn(h3, s0, lb_logits, norm_g, layer):
    b, t, _ = h3.shape
    c = 128 if t % 128 == 0 else t
    tt = _tile(t, 1024)
    assert tt % c == 0
    nch = tt // c
    nlev = int(math.log2(c))
    lm, pm = _hgrn_lmats(c)
    depth = lb_logits.shape[0]
    nh = HGRN_HEADS_PER_STEP
    w = nh * DK_C
    assert H_C % nh == 0 and DK_C == DV_C
    col = lambda off: (lambda i, h, n: (i, n, off // w + h))
    z, sn = pl.pallas_call(
        functools.partial(_hgrn_kernel, c=c, nch=nch, layer=layer, nlev=nlev), grid=(b, H_C // nh, t // tt),
        in_specs=[pl.BlockSpec((None, tt, w), col(OFF_CQ)),
                  pl.BlockSpec((None, tt, w), col(OFF_CF)),
                  pl.BlockSpec((None, tt, w), col(OFF_CI)),
                  pl.BlockSpec((None, tt, w), col(OFF_CG)),
                  pl.BlockSpec((depth, w), lambda i, h, n: (0, h)),
                  pl.BlockSpec((1, w), lambda i, h, n: (0, h)),
                  pl.BlockSpec((None, nh, DK_C, DV_C), lambda i, h, n: (i, h, 0, 0)),
                  pl.BlockSpec(lm.shape, lambda i, h, n: (0, 0)),
                  pl.BlockSpec(pm.shape, lambda i, h, n: (0, 0, 0))],
        out_specs=[pl.BlockSpec((None, tt, w), lambda i, h, n: (i, n, h)),
                   pl.BlockSpec((None, nh, DK_C, DV_C), lambda i, h, n: (i, h, 0, 0))],
        out_shape=[jax.ShapeDtypeStruct((b, t, H_C * DV_C), BF16),
                   jax.ShapeDtypeStruct((b, H_C, DK_C, DV_C), F32)],
        scratch_shapes=[pltpu.VMEM((nh, DV_C, DK_C), F32)],
        compiler_params=_cp(("parallel", "parallel", "arbitrary")), name="hgrn2",
    )(h3, h3, h3, h3, lb_logits, norm_g.reshape(1, H_C * DV_C), s0, lm, pm)
    return z, sn


def _merge_kernel(za_ref, zb_ref, zc_ref, ga_ref, gb_ref, gc_ref, x_ref, wa_ref, wb_ref, wc_ref, wo_ref,
                  g2_ref, x1_ref, xn_ref):
    ya = jnp.dot(za_ref[...], wa_ref[...], preferred_element_type=F32)
    yb = jnp.dot(zb_ref[...], wb_ref[...], preferred_element_type=F32)
    yc = jnp.dot(zc_ref[...], wc_ref[...], preferred_element_type=F32)
    y = _sigmoid(ga_ref[...]) * ya + _sigmoid(gb_ref[...]) * yb + _sigmoid(gc_ref[...]) * yc
    x1 = x_ref[...] + jnp.dot(y.astype(BF16), wo_ref[...], preferred_element_type=F32)
    x1_ref[...] = x1
    xn = x1 * lax.rsqrt(jnp.mean(x1 * x1, axis=-1, keepdims=True) + EPS) * g2_ref[...]
    xn_ref[...] = xn.astype(xn_ref.dtype)


def _merge(za, zb, zc, h2, x2d, wa, wb, wc, wo, g2):
    m, d = x2d.shape
    tm = _tile(m, 512)
    act = lambda: pl.BlockSpec((tm, d), lambda i: (i, 0))
    gate = lambda k: pl.BlockSpec((tm, d), lambda i: (i, OFF_MG // d + k))
    wgt = lambda: pl.BlockSpec((d, d), lambda i: (0, 0))
    return pl.pallas_call(
        _merge_kernel, grid=(m // tm,),
        in_specs=[act(), act(), act(), gate(0), gate(1), gate(2), act(), wgt(), wgt(), wgt(), wgt(),
                  pl.BlockSpec((1, d), lambda i: (0, 0))],
        out_specs=[act(), act()],
        out_shape=[jax.ShapeDtypeStruct((m, d), F32), jax.ShapeDtypeStruct((m, d), BF16)],
        compiler_params=_cp(("parallel",)), name="merge",
    )(za, zb, zc, h2, h2, h2, x2d, wa, wb, wc, wo, g2.reshape(1, d))


def _ffn_kernel(ug_ref, uv_ref, hg_ref, hv_ref, pg_ref, pv_ref, cw_ref, wd_ref, x1_ref, gn_ref,
                x2_ref, xn_ref, st_ref, fg_sc, fv_sc, *, ktaps, tt):
    t = pl.program_id(1)
    dff = ug_ref.shape[1]
    fg_sc[0:FFN_HALO, :] = jnp.where(t == 0, pg_ref[...], hg_ref[...])
    fv_sc[0:FFN_HALO, :] = jnp.where(t == 0, pv_ref[...], hv_ref[...])
    fg_sc[FFN_HALO:FFN_HALO + tt, :] = ug_ref[...]
    fv_sc[FFN_HALO:FFN_HALO + tt, :] = uv_ref[...]
    gate = jnp.zeros((tt, dff), F32)
    val = jnp.zeros((tt, dff), F32)
    for j in range(ktaps):
        off = FFN_HALO - (ktaps - 1) + j
        gate = gate + cw_ref[j:j + 1, 0:dff] * fg_sc[pl.ds(off, tt), :]
        val = val + cw_ref[j:j + 1, dff:2 * dff] * fv_sc[pl.ds(off, tt), :]
    act = (gate * _sigmoid(gate) * val).astype(BF16)
    x2 = x1_ref[...] + jnp.dot(act, wd_ref[...], preferred_element_type=F32)
    x2_ref[...] = x2
    xn = x2 * lax.rsqrt(jnp.mean(x2 * x2, axis=-1, keepdims=True) + EPS) * gn_ref[...]
    xn_ref[...] = xn.astype(xn_ref.dtype)
    st_ref[:, 0:dff] = fg_sc[tt:tt + FFN_HALO, :]
    st_ref[:, dff:2 * dff] = fv_sc[tt:tt + FFN_HALO, :]


def _ffn_tail(u3, prev, conv_w, w_down, x1_3, g_next, xn_dtype):
    b, t, two_dff = u3.shape
    dff = two_dff // 2
    d = x1_3.shape[2]
    ktaps = conv_w.shape[0]
    assert ktaps - 1 <= FFN_HALO
    tt = _tile(t, 256)
    nt = t // tt
    ratio = tt // FFN_HALO
    halo = lambda k: (lambda i, j: (i, jnp.maximum(j * ratio - 1, 0), k))
    x2, xn, st = pl.pallas_call(
        functools.partial(_ffn_kernel, ktaps=ktaps, tt=tt), grid=(b, nt),
        in_specs=[pl.BlockSpec((None, tt, dff), lambda i, j: (i, j, 0)),
                  pl.BlockSpec((None, tt, dff), lambda i, j: (i, j, 1)),
                  pl.BlockSpec((None, FFN_HALO, dff), halo(0)),
                  pl.BlockSpec((None, FFN_HALO, dff), halo(1)),
                  pl.BlockSpec((None, FFN_HALO, dff), lambda i, j: (i, 0, 0)),
                  pl.BlockSpec((None, FFN_HALO, dff), lambda i, j: (i, 0, 1)),
                  pl.BlockSpec((ktaps, two_dff), lambda i, j: (0, 0)),
                  pl.BlockSpec((dff, d), lambda i, j: (0, 0)),
                  pl.BlockSpec((None, tt, d), lambda i, j: (i, j, 0)),
                  pl.BlockSpec((1, d), lambda i, j: (0, 0))],
        out_specs=[pl.BlockSpec((None, tt, d), lambda i, j: (i, j, 0)),
                   pl.BlockSpec((None, tt, d), lambda i, j: (i, j, 0)),
                   pl.BlockSpec((None, FFN_HALO, two_dff), lambda i, j: (i, 0, 0))],
        out_shape=[jax.ShapeDtypeStruct((b, t, d), F32), jax.ShapeDtypeStruct((b, t, d), xn_dtype),
                   jax.ShapeDtypeStruct((b, FFN_HALO, two_dff), F32)],
        scratch_shapes=[pltpu.VMEM((FFN_HALO + tt, dff), F32), pltpu.VMEM((FFN_HALO + tt, dff), F32)],
        compiler_params=_cp(("parallel", "arbitrary")), name="ffn_tail",
    )(u3, u3, u3, u3, prev, prev, conv_w, w_down, x1_3, g_next.reshape(1, d))
    return x2, xn, st[:, FFN_HALO - (ktaps - 1):]


def _front_pad(x, rows):
    return jnp.pad(x, ((0, 0), (rows - x.shape[1], 0), (0, 0)))


def _layer(x3, xn2d, lw, layer, conv_a_prev, hg_prev, ffn_prev, nsa_fn, g_next, xn_dtype):
    b, t, d = x3.shape
    h2 = _matmul(xn2d, lw['w_in'], 2048, "in_proj")
    h3 = h2.reshape(b, t, N_PACK)
    gw = KVH * HD
    kv_rows = h3[:, :, OFF_KV:OFF_KV + 4 * gw].reshape(b, t, 4, KVH, HD)
    wrows = min(WINDOW, t)
    kv_win = h3[:, t - wrows:, OFF_KV + 4 * gw:OFF_KV + 6 * gw].reshape(b, wrows, 2, KVH, HD)

    za, conv_a_new = _conformer(h3, _front_pad(conv_a_prev, CONV_HALO), lw['conv_a_w'], lw['conv_a_b'],
                                lw['ln_a_g'], lw['ln_a_b'])
    zb = nsa_fn(h3)
    zc, hg_new = _hgrn(h3, hg_prev, lw['lb_logits'], lw['hg_norm_g'], layer)
    x1, xn2 = _merge(za.reshape(b * t, d), zb.reshape(b * t, d), zc.reshape(b * t, d), h2,
                     x3.reshape(b * t, d), lw['w_a_out'], lw['w_b_out'], lw['w_c_out'], lw['w_out'],
                     lw['norm2_g'])
    dff2 = lw['w_up'].shape[1]
    u2 = _matmul(xn2, lw['w_up'], dff2 // 2, "up_proj")
    x2, xn_next, ffn_new = _ffn_tail(u2.reshape(b, t, dff2), _front_pad(ffn_prev, FFN_HALO), lw['conv_f_w'],
                                     lw['w_down'], x1.reshape(b, t, d), g_next, xn_dtype)
    return x2, xn_next, kv_rows, kv_win, conv_a_new, hg_new, ffn_new


def kernel(x_prompt, x_sample, cache_nsa_kv, page_table, state_win_kv, state_conv_a, state_hgrn,
           state_ffn_conv, norm1_g, w_in, conv_a_w, conv_a_b, ln_a_g, ln_a_b, w_a_out, cmp_pe, cmp_w,
           w_b_out, hg_lb_logits, hg_norm_g, w_c_out, w_out, norm2_g, w_up, conv_f_w, w_down, final_g):
    depth = w_in.shape[0]
    bp, tp, d = x_prompt.shape
    bs, ts, _ = x_sample.shape
    assert d == 1024 and w_in.shape[2] == sum(IN_WIDTHS)
    cache_t = jnp.transpose(cache_nsa_kv, (0, 1, 3, 4, 5, 2))
    ka = conv_a_w.shape[1]
    kf = conv_f_w.shape[1]
    dff2 = w_up.shape[2]

    xp, xs = x_prompt, x_sample
    xnp = _rmsnorm(xp.reshape(bp * tp, d), norm1_g[0], BF16)
    xns = _rmsnorm(xs.reshape(bs * ts, d), norm1_g[0], BF16)
    outs = [[] for _ in range(10)]
    for l in range(depth):
        lw = {'w_in': _pack_w_in(w_in[l]), 'conv_a_w': conv_a_w[l], 'conv_a_b': conv_a_b[l],
              'ln_a_g': ln_a_g[l], 'ln_a_b': ln_a_b[l], 'w_a_out': w_a_out[l].astype(BF16),
              'w_b_out': w_b_out[l].astype(BF16), 'lb_logits': hg_lb_logits, 'hg_norm_g': hg_norm_g[l],
              'w_c_out': w_c_out[l].astype(BF16), 'w_out': w_out[l].astype(BF16), 'norm2_g': norm2_g[l],
              'w_up': w_up[l].astype(BF16), 'conv_f_w': conv_f_w[l], 'w_down': w_down[l].astype(BF16)}
        last = l == depth - 1
        g_next = final_g if last else norm1_g[l + 1]
        xn_dtype = F32 if last else BF16
        bd = _blockdiag_cmp_w(cmp_w[l])
        pe_t = _tile_pe(cmp_pe[l])
        nsa_p = functools.partial(_nsa_prompt, pe_t=pe_t, bd=bd)
        nsa_s = functools.partial(_nsa_sample, cache_t=cache_t, page_table=page_table, layer=l,
                                  win_prev=state_win_kv[l], pe=cmp_pe[l], bd=bd)
        xp, xnp, kv_p, win_p, ca_p, hg_p, ff_p = _layer(
            xp, xnp, lw, l, jnp.zeros((bp, ka - 1, W_A), F32), jnp.zeros((bp, H_C, DK_C, DV_C), F32),
            jnp.zeros((bp, kf - 1, dff2), F32), nsa_p, g_next, xn_dtype)
        xs, xns, kv_s, win_s, ca_s, hg_s, ff_s = _layer(
            xs, xns, lw, l, state_conv_a[l], state_hgrn[l], state_ffn_conv[l], nsa_s, g_next, xn_dtype)
        xnp = xnp.reshape(bp * tp, d)
        xns = xns.reshape(bs * ts, d)
        win_s_all = jnp.concatenate([state_win_kv[l], win_s], axis=1)
        for lst, v in zip(outs, (kv_p, kv_s, win_p, win_s_all[:, ts:], ca_p, ca_s, hg_p, hg_s, ff_p, ff_s)):
            lst.append(v)
    return (xnp.reshape(bp, tp, d), xns.reshape(bs, ts, d)) + tuple(jnp.stack(v) for v in outs)
```

```python
import functools
import math

import jax
import jax.numpy as jnp
import numpy as np
from jax import lax
from jax.experimental import pallas as pl
from jax.experimental.pallas import tpu as pltpu

F32 = jnp.float32
BF16 = jnp.bfloat16

H_B = 16
KVH = 4
HD = 64
HPG = H_B // KVH
L_CMP = 32
L_SEL = 64
N_SEL = 16
WINDOW = 512
H_C = 8
DK_C = 128
DV_C = 128
EPS = 1e-6
NEG = -1e30
BIG = 1e4
LB_TINY = 1e-30
SCALE = HD ** -0.5

LANE = 128
VMEM_LIMIT = 56 * 1024 * 1024

CONV_HALO = 32
FFN_HALO = 8
PAGES_PER_STEP = 32
NSB_PAD = 128
ROW_BLOCK = 256


def _cp(sem, vmem=VMEM_LIMIT):
    return pltpu.CompilerParams(dimension_semantics=sem, vmem_limit_bytes=vmem)


def _tile(n, pref):
    t = min(n, pref)
    while n % t:
        t -= 8
    assert t > 0
    return t


def _sigmoid(x):
    return 1.0 / (1.0 + jnp.exp(-x))


W_A = 1024
OFF_A = 0
OFF_GT = OFF_A + W_A
OFF_Q = OFF_GT + W_A
OFF_MG = OFF_Q + H_B * HD
OFF_CQ = OFF_MG + 3 * 1024
OFF_CF = OFF_CQ + H_C * DK_C
OFF_CI = OFF_CF + H_C * DK_C
OFF_CG = OFF_CI + H_C * DV_C
OFF_KV = OFF_CG + H_C * DV_C
OFF_BG = OFF_KV + 6 * KVH * HD
N_PACK = OFF_BG + KVH * LANE
IN_WIDTHS = (2 * W_A, H_B * HD, 6 * KVH * HD, 3 * H_B, H_C * DK_C, H_C * DK_C, H_C * DV_C, H_C * DV_C, 3 * 1024)


def _pack_w_in(w):
    d = w.shape[0]
    cuts = [int(c) for c in np.cumsum(IN_WIDTHS)[:-1]]
    a_in, b_q, b_kv, b_g, c_q, c_f, c_i, c_g, m_g = jnp.split(w, cuts, axis=1)
    bg = b_g.reshape(d, KVH, HPG * 3)
    bg = jnp.pad(bg, ((0, 0), (0, 0), (0, LANE - HPG * 3))).reshape(d, KVH * LANE)
    return jnp.concatenate([a_in, b_q, m_g, c_q, c_f, c_i, c_g, b_kv, bg], axis=1).astype(BF16)


def _rmsnorm_kernel(x_ref, g_ref, o_ref):
    x = x_ref[...]
    y = x * lax.rsqrt(jnp.mean(x * x, axis=-1, keepdims=True) + EPS)
    o_ref[...] = (y * g_ref[...]).astype(o_ref.dtype)


def _rmsnorm(x2d, g, out_dtype):
    m, d = x2d.shape
    tm = _tile(m, 512)
    return pl.pallas_call(
        _rmsnorm_kernel, grid=(m // tm,),
        in_specs=[pl.BlockSpec((tm, d), lambda i: (i, 0)), pl.BlockSpec((1, d), lambda i: (0, 0))],
        out_specs=pl.BlockSpec((tm, d), lambda i: (i, 0)),
        out_shape=jax.ShapeDtypeStruct((m, d), out_dtype),
        compiler_params=_cp(("parallel",)), name="rmsnorm")(x2d, g.reshape(1, d))


def _matmul_kernel(x_ref, w_ref, o_ref):
    o_ref[...] = jnp.dot(x_ref[...], w_ref[...], preferred_element_type=F32)


def _matmul(x, w, tn, name):
    m, k = x.shape
    n = w.shape[1]
    tm = _tile(m, 1024)
    assert n % tn == 0
    return pl.pallas_call(
        _matmul_kernel, grid=(n // tn, m // tm),
        in_specs=[pl.BlockSpec((tm, k), lambda j, i: (i, 0)), pl.BlockSpec((k, tn), lambda j, i: (0, j))],
        out_specs=pl.BlockSpec((tm, tn), lambda j, i: (i, j)),
        out_shape=jax.ShapeDtypeStruct((m, n), F32),
        compiler_params=_cp(("parallel", "parallel")), name=name)(x, w)


def _conformer_kernel(a_ref, g_ref, ah_ref, gh_ref, prev_ref, w_ref, b_ref, lg_ref, lb_ref,
                      z_ref, st_ref, full_ref, y_ref, *, ktaps, tt):
    t = pl.program_id(1)
    u_halo = ah_ref[...] * _sigmoid(gh_ref[...])
    full_ref[0:CONV_HALO, :] = jnp.where(t == 0, prev_ref[...], u_halo)
    full_ref[CONV_HALO:CONV_HALO + tt, :] = a_ref[...] * _sigmoid(g_ref[...])
    width = y_ref.shape[1]

    def lane_chunk(c, carry):
        cs = pl.ds(pl.multiple_of(c * LANE, LANE), LANE)
        acc = jnp.broadcast_to(b_ref[:, cs], (tt, LANE))
        for j in range(ktaps):
            acc = acc + w_ref[j:j + 1, cs] * full_ref[pl.ds(CONV_HALO - (ktaps - 1) + j, tt), cs]
        y_ref[:, cs] = acc
        return carry

    lax.fori_loop(0, width // LANE, lane_chunk, 0)
    y = y_ref[...]
    xc = y - jnp.mean(y, axis=-1, keepdims=True)
    yn = xc * lax.rsqrt(jnp.mean(xc * xc, axis=-1, keepdims=True) + EPS) * lg_ref[...] + lb_ref[...]
    z_ref[...] = (yn * _sigmoid(yn)).astype(z_ref.dtype)
    st_ref[...] = full_ref[tt:tt + CONV_HALO, :]


def _conformer(h3, prev, conv_w, conv_b, ln_g, ln_b):
    b, t, _ = h3.shape
    ktaps = conv_w.shape[0]
    assert ktaps - 1 <= CONV_HALO
    tt = _tile(t, 256)
    nt = t // tt
    if t >= CONV_HALO:
        assert tt % CONV_HALO == 0
        halo_src = h3
        ratio = tt // CONV_HALO
        halo_a = lambda i, j: (i, jnp.maximum(j * ratio - 1, 0), OFF_A // W_A)
        halo_g = lambda i, j: (i, jnp.maximum(j * ratio - 1, 0), OFF_GT // W_A)
    else:
        assert nt == 1
        halo_src = jnp.zeros((b, CONV_HALO, W_A), F32)
        halo_a = lambda i, j: (i, 0, 0)
        halo_g = halo_a
    row = lambda v: v.reshape(1, W_A)
    z, st = pl.pallas_call(
        functools.partial(_conformer_kernel, ktaps=ktaps, tt=tt), grid=(b, nt),
        in_specs=[pl.BlockSpec((None, tt, W_A), lambda i, j: (i, j, OFF_A // W_A)),
                  pl.BlockSpec((None, tt, W_A), lambda i, j: (i, j, OFF_GT // W_A)),
                  pl.BlockSpec((None, CONV_HALO, W_A), halo_a),
                  pl.BlockSpec((None, CONV_HALO, W_A), halo_g),
                  pl.BlockSpec((None, CONV_HALO, W_A), lambda i, j: (i, 0, 0)),
                  pl.BlockSpec((ktaps, W_A), lambda i, j: (0, 0)),
                  pl.BlockSpec((1, W_A), lambda i, j: (0, 0)),
                  pl.BlockSpec((1, W_A), lambda i, j: (0, 0)),
                  pl.BlockSpec((1, W_A), lambda i, j: (0, 0))],
        out_specs=[pl.BlockSpec((None, tt, W_A), lambda i, j: (i, j, 0)),
                   pl.BlockSpec((None, CONV_HALO, W_A), lambda i, j: (i, 0, 0))],
        out_shape=[jax.ShapeDtypeStruct((b, t, W_A), BF16), jax.ShapeDtypeStruct((b, CONV_HALO, W_A), F32)],
        scratch_shapes=[pltpu.VMEM((CONV_HALO + tt, W_A), F32), pltpu.VMEM((tt, W_A), F32)],
        compiler_params=_cp(("parallel", "arbitrary")), name="conformer",
    )(h3, h3, halo_src, halo_src, prev, conv_w, row(conv_b), row(ln_g), row(ln_b))
    return z, st[:, CONV_HALO - (ktaps - 1):]


def _blockdiag_cmp_w(cw):
    eye = jnp.eye(KVH, dtype=cw.dtype)
    bd = jnp.einsum('klde,gh->klgdhe', cw, eye)
    return bd.reshape(2, L_CMP * KVH * HD, KVH * HD).astype(BF16)


def _tile_pe(pe):
    return jnp.broadcast_to(pe[:, :, None, :], (2, L_CMP, KVH, HD)).reshape(2, 1, L_CMP * KVH * HD)


def _compress_kernel(x_ref, pe_ref, w_ref, o_ref):
    x = (x_ref[...] + pe_ref[...]).astype(BF16)
    o_ref[...] = jnp.dot(x, w_ref[...], preferred_element_type=F32)


def _compress(x, pe_t, bd):
    _, r, kdim = x.shape
    n = KVH * HD
    tr = _tile(r, 256)
    return pl.pallas_call(
        _compress_kernel, grid=(2, r // tr),
        in_specs=[pl.BlockSpec((None, tr, kdim), lambda s, i: (s, i, 0)),
                  pl.BlockSpec((None, 1, kdim), lambda s, i: (s, 0, 0)),
                  pl.BlockSpec((None, kdim, n), lambda s, i: (s, 0, 0))],
        out_specs=pl.BlockSpec((None, tr, n), lambda s, i: (s, i, 0)),
        out_shape=jax.ShapeDtypeStruct((2, r, n), F32),
        compiler_params=_cp(("parallel", "parallel")), name="nsa_compress")(x, pe_t, bd)


def _even_odd(x, axis):
    n = x.shape[axis]
    y = x.reshape(x.shape[:axis] + (n // 2, 2) + x.shape[axis + 1:])
    return jnp.swapaxes(y, axis, axis + 1).reshape(x.shape)


def _rank_select_rows(score, k):
    n = score.shape[0]
    jrow = lax.broadcasted_iota(jnp.int32, score.shape, 0)
    cnt = jnp.zeros(score.shape, F32)
    for i in range(n):
        si = score[i:i + 1, :]
        ge = jnp.where(si >= score, 1.0, 0.0)
        gt = jnp.where(si > score, 1.0, 0.0)
        cnt = cnt + jnp.where(jrow > i, ge, gt)
    return jnp.where(cnt < k, 1.0, 0.0)


def _topk_lanes(score, k):
    n = score.shape[-1]
    cidx = lax.broadcasted_iota(jnp.int32, score.shape, score.ndim - 1).astype(F32)
    sel = jnp.zeros(score.shape, F32)
    s = score
    for _ in range(k):
        m = jnp.max(s, axis=-1, keepdims=True)
        imin = jnp.min(jnp.where(s == m, cidx, float(n)), axis=-1, keepdims=True)
        hit = cidx == imin
        sel = jnp.where(hit, 1.0, sel)
        s = jnp.where(hit, -jnp.inf, s)
    return sel


_NT = (((1,), (1,)), ((), ()))
_TN = (((0,), (0,)), ((), ()))


def _ones_lanes(v):
    return jnp.concatenate([v, jnp.ones(v.shape[:-1] + (LANE - v.shape[-1],), v.dtype)], axis=-1)


def _flash_update(m_prev, acc_prev, s, v_aug, nt=False):
    m_new = jnp.maximum(m_prev, jnp.max(s, axis=1, keepdims=True))
    alpha = jnp.exp(m_prev - m_new)
    p = jnp.exp(s - m_new[:, 0:1]).astype(BF16)
    if nt:
        pv = lax.dot_general(p, v_aug, _NT, preferred_element_type=F32)
    else:
        pv = jnp.dot(p, v_aug, preferred_element_type=F32)
    return m_new, acc_prev * alpha + pv


def _flash_step(j, s, v_aug, m_sc, acc_sc, nt=False):
    m_sc[j], acc_sc[j] = _flash_update(m_sc[j], acc_sc[j], s, v_aug, nt)


def _flash_reset(m_sc, acc_sc):
    m_sc[...] = jnp.full(m_sc.shape, NEG, F32)
    acc_sc[...] = jnp.zeros(acc_sc.shape, F32)


def _flash_out(acc):
    return acc[:, 0:HD] / acc[:, HD:2 * HD]


def _nsa_prompt_kernel(q_ref, bg_ref, kc_ref, vc_ref, ks_ref, vs_ref, kw_ref, vw_ref, ex_ref, o_ref,
                       q_sc, selm_sc, m_sc, acc_sc, win_sc, bias_sc, *, tq, tk, wk, nb, nsb):
    t0 = pl.program_id(2) * tq
    q = (q_ref[...] * SCALE).astype(BF16)
    for j in range(HPG):
        q_sc[j] = q[:, j * HD:(j + 1) * HD]
    kc = kc_ref[...]
    vc = vc_ref[...]

    half = nb // 2
    row = lax.broadcasted_iota(jnp.int32, (nb, tq), 0)
    tpos = t0 + lax.broadcasted_iota(jnp.int32, (nb, tq), 1)
    blk = jnp.where(row < half, 2 * row, 2 * (row - half) + 1)
    cvalid = (blk + 1) * L_CMP - 1 <= tpos
    sts = [lax.dot_general(kc, q_sc[j], _NT, preferred_element_type=F32) for j in range(HPG)]
    pcs = []
    for j in range(HPG):
        sm = jnp.where(cvalid, sts[j], NEG)
        e = jnp.exp(sm - jnp.max(sm, axis=0, keepdims=True))
        pcs.append(jnp.where(cvalid, e / jnp.sum(e, axis=0, keepdims=True), 0.0))
    o_cmp = [lax.dot_general(pcs[j].astype(BF16), vc, _TN, preferred_element_type=F32) for j in range(HPG)]
    pg = pcs[0]
    for j in range(1, HPG):
        pg = pg + pcs[j]

    ps = pg[:half] + pg[half:]
    jrow = lax.broadcasted_iota(jnp.int32, (nsb, tq), 0)
    tp = t0 + lax.broadcasted_iota(jnp.int32, (nsb, tq), 1)
    cur = tp // L_SEL
    forced_ps = jnp.where(jrow == 0, BIG, jnp.where(jrow == cur, BIG, jnp.where(jrow == cur - 1, BIG, ps)))
    score = jnp.where(jrow * L_SEL <= tp, forced_ps, -2.0 * BIG)
    sel = _rank_select_rows(score, min(N_SEL, nsb))
    sel = jnp.concatenate([sel, jnp.zeros((NSB_PAD - nsb, tq), F32)], axis=0)
    selm_sc[...] = jnp.transpose(sel).astype(BF16)

    _flash_reset(m_sc, acc_sc)
    nrb = tq // ROW_BLOCK

    def sel_chunk(c, carry):
        k0 = pl.multiple_of(c * tk, tk)
        chosen = jnp.dot(selm_sc[...], ex_ref[:, pl.ds(k0, tk)], preferred_element_type=F32)
        qpos = t0 + lax.broadcasted_iota(jnp.int32, (tq, tk), 0)
        kpos = k0 + lax.broadcasted_iota(jnp.int32, (tq, tk), 1)
        bias_sc[:, 0:tk] = jnp.where(kpos <= qpos, (chosen - 1.0) * (-NEG), NEG)

        def rows(rb, carry2):
            rs = pl.ds(pl.multiple_of(rb * ROW_BLOCK, ROW_BLOCK), ROW_BLOCK)
            ks = ks_ref[pl.ds(k0, tk), :]
            vs = vs_ref[pl.ds(k0, tk), :]
            bias = bias_sc[rs, 0:tk]
            heads = range(HPG)
            ss = [(lax.dot_general(q_sc[j, rs, :], ks, _NT, preferred_element_type=F32) + bias).astype(BF16)
                  for j in heads]
            m_prev = [m_sc[j, rs, :] for j in heads]
            m_new = [jnp.maximum(m_prev[j], jnp.max(ss[j], axis=1, keepdims=True).astype(F32)) for j in heads]
            ps = [jnp.exp(ss[j] - m_new[j][:, 0:1].astype(BF16)) for j in heads]
            pvs = [jnp.dot(ps[j], vs, preferred_element_type=F32) for j in heads]
            for j in heads:
                acc_sc[j, rs, :] = acc_sc[j, rs, :] * jnp.exp(m_prev[j] - m_new[j]) + pvs[j]
                m_sc[j, rs, :] = m_new[j]
            return carry2

        lax.fori_loop(0, nrb, rows, 0)
        return carry

    lax.fori_loop(0, (t0 + tq + tk - 1) // tk, sel_chunk, 0)

    w0 = pl.multiple_of(jnp.maximum(t0 - WINDOW, 0), LANE)
    d = (t0 + lax.broadcasted_iota(jnp.int32, (tq, wk), 0)) - (w0 + lax.broadcasted_iota(jnp.int32, (tq, wk), 1))
    bias_sc[:, 0:wk] = jnp.where(d >= 0, jnp.where(d < WINDOW, 0.0, NEG), NEG)

    def win_rows(rb, carry):
        rs = pl.ds(pl.multiple_of(rb * ROW_BLOCK, ROW_BLOCK), ROW_BLOCK)
        kw = kw_ref[pl.ds(w0, wk), :]
        vw = vw_ref[pl.ds(w0, wk), :]
        bias = bias_sc[rs, 0:wk]
        heads = range(HPG)
        ss = [(lax.dot_general(q_sc[j, rs, :], kw, _NT, preferred_element_type=F32) + bias).astype(BF16)
              for j in heads]
        ps = [jnp.exp(ss[j] - jnp.max(ss[j], axis=1, keepdims=True)) for j in heads]
        pvs = [jnp.dot(ps[j], vw, preferred_element_type=F32) for j in heads]
        for j in heads:
            win_sc[j, rs, :] = pvs[j]
        return carry

    lax.fori_loop(0, nrb, win_rows, 0)

    sg = _sigmoid(bg_ref[...])
    outs = []
    for j in range(HPG):
        outs.append(sg[:, 3 * j:3 * j + 1] * o_cmp[j] + sg[:, 3 * j + 1:3 * j + 2] * _flash_out(acc_sc[j])
                    + sg[:, 3 * j + 2:3 * j + 3] * _flash_out(win_sc[j]))
    o_ref[...] = jnp.concatenate(outs, axis=1).astype(o_ref.dtype)


def _nsa_prompt(h3, pe_t, bd):
    b, t, _ = h3.shape
    tq, tk = 256, 512
    wk = WINDOW + tq
    assert t % tk == 0 and t % L_SEL == 0 and t >= wk and WINDOW % LANE == 0 and tq % ROW_BLOCK == 0
    nb = t // L_CMP
    nsb = t // L_SEL
    assert nsb <= NSB_PAD and nb == 2 * nsb
    kv = h3[:, :, OFF_KV:OFF_KV + 6 * KVH * HD]
    xc = jnp.stack([kv[:, :, 0:KVH * HD].reshape(b * nb, L_CMP * KVH * HD),
                    kv[:, :, KVH * HD:2 * KVH * HD].reshape(b * nb, L_CMP * KVH * HD)])
    cmp = _compress(xc, pe_t, bd).reshape(2, b, nb, KVH, HD)
    cmp = jnp.transpose(_even_odd(cmp, 2), (0, 1, 3, 2, 4)).astype(BF16)
    kvh = jnp.transpose(kv[:, :, 2 * KVH * HD:].reshape(b, t, 4, KVH, HD), (2, 0, 3, 1, 4)).astype(BF16)
    expand = jnp.asarray(np.arange(NSB_PAD)[:, None] == np.arange(t)[None, :] // L_SEL, dtype=BF16)
    qcol = OFF_Q // (HPG * HD)
    bgcol = OFF_BG // LANE
    head = lambda w: pl.BlockSpec((None, None, t, w), lambda i, g, n: (i, g, 0, 0))
    return pl.pallas_call(
        functools.partial(_nsa_prompt_kernel, tq=tq, tk=tk, wk=wk, nb=nb, nsb=nsb), grid=(b, KVH, t // tq),
        in_specs=[pl.BlockSpec((None, tq, HPG * HD), lambda i, g, n: (i, n, qcol + g)),
                  pl.BlockSpec((None, tq, LANE), lambda i, g, n: (i, n, bgcol + g)),
                  pl.BlockSpec((None, None, nb, HD), lambda i, g, n: (i, g, 0, 0)),
                  pl.BlockSpec((None, None, nb, HD), lambda i, g, n: (i, g, 0, 0)),
                  head(HD), head(LANE), head(HD), head(LANE),
                  pl.BlockSpec((NSB_PAD, t), lambda i, g, n: (0, 0))],
        out_specs=pl.BlockSpec((None, tq, HPG * HD), lambda i, g, n: (i, n, g)),
        out_shape=jax.ShapeDtypeStruct((b, t, H_B * HD), BF16),
        scratch_shapes=[pltpu.VMEM((HPG, tq, HD), BF16), pltpu.VMEM((tq, NSB_PAD), BF16),
                        pltpu.VMEM((HPG, tq, LANE), F32), pltpu.VMEM((HPG, tq, LANE), F32),
                        pltpu.VMEM((HPG, tq, LANE), F32), pltpu.VMEM((tq, max(tk, wk)), F32)],
        compiler_params=_cp(("parallel", "parallel", "arbitrary")), name="nsa_prompt",
    )(h3, h3, cmp[0], cmp[1], kvh[0], _ones_lanes(kvh[1]), kvh[2], _ones_lanes(kvh[3]), expand)


def _cmp_perm(page):
    per = page // L_CMP
    m = np.zeros((2 * page, 2 * page), np.float32)
    for l in range(L_CMP):
        for p in range(2):
            for n in range(per):
                m[l * 2 * per + p * per + n, p * page + n * L_CMP + l] = 1.0
    return jnp.asarray(m, dtype=BF16)


def _cmp_past_kernel(pt_ref, *refs, n_steps, per):
    pages = refs[:PAGES_PER_STEP]
    pe_ref, perm_ref, w_ref, o_ref, x_sc = refs[PAGES_PER_STEP:]
    pg = pl.program_id(1)
    gd = KVH * HD
    page = pages[0].shape[-1]
    rows = PAGES_PER_STEP * per
    grp = 2 * per
    for s in range(2):
        moved = []
        for pr in range(PAGES_PER_STEP // 2):
            xt = jnp.concatenate([pages[2 * pr][s].reshape(gd, page), pages[2 * pr + 1][s].reshape(gd, page)], axis=1)
            xt = (xt + pe_ref[s]).astype(BF16)
            moved.append(lax.dot_general(perm_ref[...], xt, _NT, preferred_element_type=F32))
        for l in range(L_CMP):
            slab = jnp.concatenate([m[l * grp:(l + 1) * grp] for m in moved], axis=0)
            x_sc[s, l, pl.ds(pl.multiple_of(pg * rows, rows), rows), :] = slab.astype(BF16)

    @pl.when(pg == n_steps - 1)
    def _():
        o_ref[...] = jnp.zeros(o_ref.shape, F32)

        def body(l, carry):
            for s in range(2):
                o_ref[s] += jnp.dot(x_sc[s, l], w_ref[s, l], preferred_element_type=F32)
            return carry

        lax.fori_loop(0, L_CMP, body, 0)


def _cmp_past(cache_t, page_table, layer, pe, bd):
    b, n_pages = page_table.shape
    page = cache_t.shape[-1]
    per = page // L_CMP
    gd = KVH * HD
    assert n_pages % PAGES_PER_STEP == 0 and PAGES_PER_STEP % 2 == 0 and 2 * per == 8
    n_steps = n_pages // PAGES_PER_STEP
    nbp = n_pages * per
    pe_cols = jnp.tile(jnp.swapaxes(pe, 1, 2), (1, KVH, 2 * per))
    specs = []
    for slot in range(PAGES_PER_STEP):
        def index_map(i, pg, pt, slot=slot):
            return (layer, pt[i, pg * PAGES_PER_STEP + slot], 0, 0, 0, 0)
        specs.append(pl.BlockSpec((None, None, 2, KVH, HD, page), index_map))
    grid_spec = pltpu.PrefetchScalarGridSpec(
        num_scalar_prefetch=1, grid=(b, n_steps),
        in_specs=specs + [pl.BlockSpec((2, gd, 2 * page), lambda i, pg, pt: (0, 0, 0)),
                          pl.BlockSpec((2 * page, 2 * page), lambda i, pg, pt: (0, 0)),
                          pl.BlockSpec((2, L_CMP, gd, gd), lambda i, pg, pt: (0, 0, 0, 0))],
        out_specs=pl.BlockSpec((None, 2, nbp, gd), lambda i, pg, pt: (i, 0, 0, 0)),
        scratch_shapes=[pltpu.VMEM((2, L_CMP, nbp, gd), BF16)])
    return pl.pallas_call(
        functools.partial(_cmp_past_kernel, n_steps=n_steps, per=per), grid_spec=grid_spec,
        out_shape=jax.ShapeDtypeStruct((b, 2, nbp, gd), F32),
        compiler_params=_cp(("parallel", "arbitrary")), name="nsa_cmp_past",
    )(page_table, *([cache_t] * PAGES_PER_STEP), pe_cols, _cmp_perm(page), bd.reshape(2, L_CMP, gd, gd))


def _stack_heads(x, g):
    return jnp.concatenate([x[:, (g * HPG + j) * HD:(g * HPG + j + 1) * HD] for j in range(HPG)], axis=0)


def _sample_cmp_win_kernel(q_ref, bg_ref, kc_ref, vc_ref, wp_ref, wn_ref, ocw_ref, sel_ref,
                           *, t, past, nbp, nsb, sel_pad):
    q = (q_ref[...] * SCALE).astype(BF16)
    sg = _sigmoid(bg_ref[...])
    rows = HPG * t
    half = nbp // 2
    wb = wp_ref.shape[0]
    gw = KVH * HD

    col = lax.broadcasted_iota(jnp.int32, (rows, nbp), 1)
    tq = lax.broadcasted_iota(jnp.int32, (rows, nbp), 0) % t
    blk = jnp.where(col < half, 2 * col, 2 * (col - half) + 1)
    cvalid = (blk + 1) * L_CMP - 1 <= past + tq

    wi = lax.broadcasted_iota(jnp.int32, (rows, wb), 1)
    wt = lax.broadcasted_iota(jnp.int32, (rows, wb), 0) % t
    d_prev = wt + wb - wi
    ni = lax.broadcasted_iota(jnp.int32, (rows, t), 1)
    nt = lax.broadcasted_iota(jnp.int32, (rows, t), 0) % t
    d_new = nt - ni

    ps_all = []
    outs = [None] * H_B
    for g in range(KVH):
        qg = _stack_heads(q, g)
        kc = kc_ref[:, g * HD:(g + 1) * HD].astype(BF16)
        vc = vc_ref[:, g * HD:(g + 1) * HD].astype(BF16)
        s = lax.dot_general(qg, kc, _NT, preferred_element_type=F32)
        sm = jnp.where(cvalid, s, NEG)
        e = jnp.exp(sm - jnp.max(sm, axis=1, keepdims=True))
        p = jnp.where(cvalid, e / jnp.sum(e, axis=1, keepdims=True), 0.0)
        o_cmp = jnp.dot(p.astype(BF16), vc, preferred_element_type=F32)
        pgrp = p[0:t]
        for j in range(1, HPG):
            pgrp = pgrp + p[j * t:(j + 1) * t]
        ps_all.append(pgrp[:, :half] + pgrp[:, half:])

        kp = wp_ref[:, g * HD:(g + 1) * HD].astype(BF16)
        vp = wp_ref[:, gw + g * HD:gw + (g + 1) * HD].astype(BF16)
        kn = wn_ref[:, g * HD:(g + 1) * HD].astype(BF16)
        vn = wn_ref[:, gw + g * HD:gw + (g + 1) * HD].astype(BF16)
        s1 = lax.dot_general(qg, kp, _NT, preferred_element_type=F32)
        s1 = jnp.where(d_prev >= 0, jnp.where(d_prev < WINDOW, s1, NEG), NEG)
        s2 = lax.dot_general(qg, kn, _NT, preferred_element_type=F32)
        s2 = jnp.where(d_new >= 0, jnp.where(d_new < WINDOW, s2, NEG), NEG)
        mx = jnp.maximum(jnp.max(s1, axis=1, keepdims=True), jnp.max(s2, axis=1, keepdims=True))
        e1 = jnp.exp(s1 - mx)
        e2 = jnp.exp(s2 - mx)
        den = jnp.sum(e1, axis=1, keepdims=True) + jnp.sum(e2, axis=1, keepdims=True)
        o_win = (jnp.dot(e1.astype(BF16), vp, preferred_element_type=F32)
                 + jnp.dot(e2.astype(BF16), vn, preferred_element_type=F32)) / den
        for j in range(HPG):
            c = g * LANE + 3 * j
            outs[g * HPG + j] = (sg[:, c:c + 1] * o_cmp[j * t:(j + 1) * t]
                                 + sg[:, c + 2:c + 3] * o_win[j * t:(j + 1) * t])
    ocw_ref[...] = jnp.concatenate(outs, axis=1)

    ps = jnp.concatenate(ps_all, axis=0)
    ps = jnp.concatenate([ps, jnp.zeros((KVH * t, sel_pad - half), F32)], axis=1)
    jb = lax.broadcasted_iota(jnp.int32, (KVH * t, sel_pad), 1)
    qp = past + lax.broadcasted_iota(jnp.int32, (KVH * t, sel_pad), 0) % t
    cur = qp // L_SEL
    forced_ps = jnp.where(jb == 0, BIG, jnp.where(jb == cur, BIG, jnp.where(jb == cur - 1, BIG, ps)))
    score = jnp.where(jb * L_SEL <= qp, forced_ps, -2.0 * BIG)
    score = jnp.where(jb < nsb, score, -jnp.inf)
    sel_ref[...] = _topk_lanes(score, min(N_SEL, nsb))


def _sample_sel_kernel(pt_ref, *refs, t, n_steps):
    pages = refs[:PAGES_PER_STEP]
    q_ref, bg_ref, selp_ref, seln_ref, ex_ref, kvn_ref, ocw_ref, o_ref, q_sc, m_sc, acc_sc = refs[PAGES_PER_STEP:]
    pg = pl.program_id(1)
    rows = HPG * t
    gw = KVH * HD
    keys = PAGES_PER_STEP * pages[0].shape[-1]

    @pl.when(pg == 0)
    def _():
        q = (q_ref[...] * SCALE).astype(BF16)
        for g in range(KVH):
            q_sc[g] = _stack_heads(q, g)
        _flash_reset(m_sc, acc_sc)

    def per_head_rows(x):
        return jnp.concatenate([x[g * t:(g + 1) * t] for g in range(KVH) for _ in range(HPG)], axis=0)

    chosen = jnp.dot(per_head_rows(selp_ref[...]).astype(BF16), ex_ref[...], preferred_element_type=F32)
    ones = jnp.ones((LANE - HD, keys), BF16)
    bias = (chosen - 1.0) * (-NEG)
    groups = range(KVH)
    ss = []
    for g in groups:
        kt = jnp.concatenate([p[0, g] for p in pages], axis=1).astype(BF16)
        ss.append(jnp.dot(q_sc[g], kt, preferred_element_type=F32))
    s = jnp.concatenate(ss, axis=0) + bias
    m_prev = m_sc[...]
    m_new = jnp.maximum(m_prev, jnp.max(s, axis=1, keepdims=True))
    p = jnp.exp(s - m_new[:, 0:1]).astype(BF16)
    pvs = []
    for g in groups:
        vt = jnp.concatenate([pp[1, g] for pp in pages], axis=1).astype(BF16)
        pvs.append(lax.dot_general(p[g * rows:(g + 1) * rows], jnp.concatenate([vt, ones], axis=0), _NT,
                                   preferred_element_type=F32))
    acc_sc[...] = acc_sc[...] * jnp.exp(m_prev - m_new) + jnp.concatenate(pvs, axis=0)
    m_sc[...] = m_new

    @pl.when(pg == n_steps - 1)
    def _():
        sg = _sigmoid(bg_ref[...])
        ni = lax.broadcasted_iota(jnp.int32, (rows, t), 1)
        nt = lax.broadcasted_iota(jnp.int32, (rows, t), 0) % t
        new_chosen = per_head_rows(seln_ref[...])[:, 0:1] > 0.5
        outs = []
        for g in range(KVH):
            rs = slice(g * rows, (g + 1) * rows)
            kn = kvn_ref[:, g * HD:(g + 1) * HD].astype(BF16)
            vn = _ones_lanes(kvn_ref[:, gw + g * HD:gw + (g + 1) * HD].astype(BF16))
            s = lax.dot_general(q_sc[g], kn, _NT, preferred_element_type=F32)
            s = jnp.where(ni <= nt, jnp.where(new_chosen[rs], s, NEG), NEG)
            _flash_step(rs, s, vn, m_sc, acc_sc)
            o_sel = _flash_out(acc_sc[rs])
            for j in range(HPG):
                c = g * LANE + 3 * j + 1
                outs.append(sg[:, c:c + 1] * o_sel[j * t:(j + 1) * t])
        o_ref[...] = (ocw_ref[...] + jnp.concatenate(outs, axis=1)).astype(o_ref.dtype)


def _nsa_sample(h3, cache_t, page_table, layer, win_prev, pe, bd):
    b, t, _ = h3.shape
    n_pages = page_table.shape[1]
    page = cache_t.shape[-1]
    past = n_pages * page
    assert t < L_CMP and t <= L_SEL and past % L_SEL == 0 and t % 8 == 0
    nbp = past // L_CMP
    n_past_blk = past // L_SEL
    nsb = -(-(past + t) // L_SEL)
    assert nsb == n_past_blk + 1
    sel_pad = -(-nsb // LANE) * LANE
    gw = KVH * HD
    cmp = _even_odd(_cmp_past(cache_t, page_table, layer, pe, bd), 2)
    wb = win_prev.shape[1]
    wp = win_prev.reshape(b, wb, 2 * gw)
    ocw, sel = pl.pallas_call(
        functools.partial(_sample_cmp_win_kernel, t=t, past=past, nbp=nbp, nsb=nsb, sel_pad=sel_pad),
        grid=(b,),
        in_specs=[pl.BlockSpec((None, t, H_B * HD), lambda i: (i, 0, OFF_Q // (H_B * HD))),
                  pl.BlockSpec((None, t, KVH * LANE), lambda i: (i, 0, OFF_BG // (KVH * LANE))),
                  pl.BlockSpec((None, None, nbp, gw), lambda i: (i, 0, 0, 0)),
                  pl.BlockSpec((None, None, nbp, gw), lambda i: (i, 1, 0, 0)),
                  pl.BlockSpec((None, wb, 2 * gw), lambda i: (i, 0, 0)),
                  pl.BlockSpec((None, t, 2 * gw), lambda i: (i, 0, (OFF_KV + 4 * gw) // (2 * gw)))],
        out_specs=[pl.BlockSpec((None, t, H_B * HD), lambda i: (i, 0, 0)),
                   pl.BlockSpec((None, KVH * t, sel_pad), lambda i: (i, 0, 0))],
        out_shape=[jax.ShapeDtypeStruct((b, t, H_B * HD), F32),
                   jax.ShapeDtypeStruct((b, KVH * t, sel_pad), F32)],
        compiler_params=_cp(("parallel",)), name="nsa_sample_cmp_win",
    )(h3, h3, cmp, cmp, wp, h3)

    n_steps = n_pages // PAGES_PER_STEP
    keys = PAGES_PER_STEP * page
    bps = keys // L_SEL
    assert keys % L_SEL == 0 and bps <= LANE
    lane_pad = lambda x: jnp.pad(x, [(0, 0)] * (x.ndim - 1) + [(0, LANE - x.shape[-1])])
    sel_past = lane_pad(jnp.swapaxes(sel[:, :, :n_past_blk].reshape(b, KVH * t, n_steps, bps), 1, 2))
    sel_new = lane_pad(sel[:, :, n_past_blk:n_past_blk + 1])
    expand = jnp.asarray(np.arange(LANE)[:, None] == np.arange(keys)[None, :] // L_SEL, dtype=BF16)
    specs = []
    for slot in range(PAGES_PER_STEP):
        def index_map(i, pg, pt, slot=slot):
            return (layer, pt[i, pg * PAGES_PER_STEP + slot], 1, 0, 0, 0)
        specs.append(pl.BlockSpec((None, None, 2, KVH, HD, page), index_map))
    nrow = KVH * HPG * t
    grid_spec = pltpu.PrefetchScalarGridSpec(
        num_scalar_prefetch=1, grid=(b, n_steps),
        in_specs=specs + [
            pl.BlockSpec((None, t, H_B * HD), lambda i, pg, pt: (i, 0, OFF_Q // (H_B * HD))),
            pl.BlockSpec((None, t, KVH * LANE), lambda i, pg, pt: (i, 0, OFF_BG // (KVH * LANE))),
            pl.BlockSpec((None, None, KVH * t, LANE), lambda i, pg, pt: (i, pg, 0, 0)),
            pl.BlockSpec((None, KVH * t, LANE), lambda i, pg, pt: (i, 0, 0)),
            pl.BlockSpec((LANE, keys), lambda i, pg, pt: (0, 0)),
            pl.BlockSpec((None, t, 2 * gw), lambda i, pg, pt: (i, 0, (OFF_KV + 2 * gw) // (2 * gw))),
            pl.BlockSpec((None, t, H_B * HD), lambda i, pg, pt: (i, 0, 0))],
        out_specs=pl.BlockSpec((None, t, H_B * HD), lambda i, pg, pt: (i, 0, 0)),
        scratch_shapes=[pltpu.VMEM((KVH, HPG * t, HD), BF16), pltpu.VMEM((nrow, LANE), F32),
                        pltpu.VMEM((nrow, LANE), F32)])
    return pl.pallas_call(
        functools.partial(_sample_sel_kernel, t=t, n_steps=n_steps),
        grid_spec=grid_spec, out_shape=jax.ShapeDtypeStruct((b, t, H_B * HD), BF16),
        compiler_params=_cp(("parallel", "arbitrary")), name="nsa_sample_sel",
    )(page_table, *([cache_t] * PAGES_PER_STEP), h3, h3, sel_past, sel_new, expand, h3, ocw)


def _hgrn_lmats(c):
    nlev = int(math.log2(c))
    assert 1 << nlev == c
    r = np.arange(c)[:, None]
    i = np.arange(c)[None, :]
    mats = [i <= r, i > r]
    pairs = []
    for lev in range(nlev):
        blk = c >> lev
        mid = (r // blk) * blk + blk // 2
        mats.append(np.where(r >= mid, (i >= mid) & (i <= r), (i > r) & (i < mid)))
        pairs.append((r // blk == i // blk) & (r % blk >= blk // 2) & (i % blk < blk // 2))
    return (jnp.asarray(np.concatenate(mats, axis=0).astype(np.float32), dtype=BF16),
            jnp.asarray(np.stack(pairs).astype(np.float32)))


HGRN_HEADS_PER_STEP = 4


def _hgrn_kernel(cq_ref, cf_ref, ci_ref, cg_ref, lbl_ref, ng_ref, s0_ref, lm_ref, pm_ref, z_ref, sn_ref, st_sc,
                 *, c, nch, layer, nlev):
    tstep = pl.program_id(2)
    nh = HGRN_HEADS_PER_STEP

    @pl.when(tstep == 0)
    def _():
        for hh in range(nh):
            st_sc[hh] = jnp.transpose(s0_ref[hh])

    lg = lbl_ref[...]
    e = jnp.exp(lg - jnp.max(lg, axis=0, keepdims=True))
    p = e / jnp.sum(e, axis=0, keepdims=True)
    cs = p[0:1]
    for i in range(1, layer + 1):
        cs = cs + p[i:i + 1]
    lb_all = cs - p[layer:layer + 1]
    lm = lm_ref[...]
    cp = max(c, LANE)

    def chunk(ci, carry):
        rs = pl.ds(pl.multiple_of(ci * c, c), c)
        heads = range(nh)
        hsl = [slice(hh * DK_C, (hh + 1) * DK_C) for hh in heads]
        fp = [cf_ref[rs, hs] for hs in hsl]
        cq = [cq_ref[rs, hs] for hs in hsl]
        v = [ci_ref[rs, hs] for hs in hsl]
        kk, qq, l2 = [], [], []
        for hh in heads:
            lb = lb_all[:, hsl[hh]]
            log_lb = jnp.log(jnp.maximum(lb, LB_TINY))
            log_sig = jnp.minimum(fp[hh], 0.0) - jnp.log(1.0 + jnp.exp(-jnp.abs(fp[hh])))
            x2 = jnp.log1p(-lb) + log_sig
            logf = jnp.where(lb > 0, jnp.maximum(log_lb, x2) + jnp.log(1.0 + jnp.exp(-jnp.abs(log_lb - x2))),
                             log_sig)
            kk.append((1.0 - lb) * _sigmoid(-fp[hh]))
            qq.append(cq[hh] * _sigmoid(cq[hh]))
            hi = logf.astype(BF16)
            l2.append(jnp.concatenate([hi, (logf - hi.astype(F32)).astype(BF16)], axis=1))
        d2 = [jnp.dot(lm, l2[hh], preferred_element_type=F32) for hh in heads]
        dd = [d[:, :DK_C] + d[:, DK_C:] for d in d2]
        st = [st_sc[hh] for hh in heads]
        o1 = [lax.dot_general((qq[hh] * jnp.exp(dd[hh][0:c])).astype(BF16), st[hh].astype(BF16), _NT,
                              preferred_element_type=F32) for hh in heads]
        al = []
        for lev in range(nlev):
            for hh in heads:
                ee = jnp.exp(dd[hh][(2 + lev) * c:(3 + lev) * c])
                al.append(lax.dot_general((qq[hh] * ee).astype(BF16), (kk[hh] * ee).astype(BF16), _NT,
                                          preferred_element_type=F32))
        amat = []
        for hh in heads:
            acc = al[hh] * pm_ref[0]
            for lev in range(1, nlev):
                acc = acc + al[lev * nh + hh] * pm_ref[lev]
            amat.append(acc.astype(BF16))
        o2 = [jnp.dot(amat[hh], v[hh].astype(BF16), preferred_element_type=F32) for hh in heads]
        upd = []
        for hh in heads:
            kd = kk[hh] * jnp.exp(dd[hh][c:2 * c])
            vp = v[hh]
            if cp > c:
                zpad = jnp.zeros((cp - c, DK_C), F32)
                kd = jnp.concatenate([kd, zpad], axis=0)
                vp = jnp.concatenate([vp, zpad], axis=0)
            upd.append(jnp.dot(jnp.transpose(vp).astype(BF16), kd.astype(BF16), preferred_element_type=F32))
        for hh in heads:
            st_sc[hh] = st[hh] * jnp.exp(dd[hh][c - 1:c, :]) + upd[hh]
            o = o1[hh] + o2[hh] + jnp.sum(qq[hh] * kk[hh], axis=1, keepdims=True) * v[hh]
            y = o * lax.rsqrt(jnp.mean(o * o, axis=1, keepdims=True) + EPS) * ng_ref[:, hsl[hh]]
            cg = cg_ref[rs, hsl[hh]]
            z_ref[rs, hsl[hh]] = (y * (cg * _sigmoid(cg))).astype(z_ref.dtype)
        return carry

    lax.fori_loop(0, nch, chunk, 0)

    @pl.when(tstep == pl.num_programs(2) - 1)
    def _():
        for hh in range(nh):
            sn_ref[hh] = jnp.transpose(st_sc[hh])


def _hgrn(h3, s0, lb_logits, norm_g, layer):
    b, t, _ = h3.shape
    c = 128 if t % 128 == 0 else t
    tt = _tile(t, 1024)
    assert tt % c == 0
    nch = tt // c
    nlev = int(math.log2(c))
    lm, pm = _hgrn_lmats(c)
    depth = lb_logits.shape[0]
    nh = HGRN_HEADS_PER_STEP
    w = nh * DK_C
    assert H_C % nh == 0 and DK_C == DV_C
    col = lambda off: (lambda i, h, n: (i, n, off // w + h))
    z, sn = pl.pallas_call(
        functools.partial(_hgrn_kernel, c=c, nch=nch, layer=layer, nlev=nlev), grid=(b, H_C // nh, t // tt),
        in_specs=[pl.BlockSpec((None, tt, w), col(OFF_CQ)),
                  pl.BlockSpec((None, tt, w), col(OFF_CF)),
                  pl.BlockSpec((None, tt, w), col(OFF_CI)),
                  pl.BlockSpec((None, tt, w), col(OFF_CG)),
                  pl.BlockSpec((depth, w), lambda i, h, n: (0, h)),
                  pl.BlockSpec((1, w), lambda i, h, n: (0, h)),
                  pl.BlockSpec((None, nh, DK_C, DV_C), lambda i, h, n: (i, h, 0, 0)),
                  pl.BlockSpec(lm.shape, lambda i, h, n: (0, 0)),
                  pl.BlockSpec(pm.shape, lambda i, h, n: (0, 0, 0))],
        out_specs=[pl.BlockSpec((None, tt, w), lambda i, h, n: (i, n, h)),
                   pl.BlockSpec((None, nh, DK_C, DV_C), lambda i, h, n: (i, h, 0, 0))],
        out_shape=[jax.ShapeDtypeStruct((b, t, H_C * DV_C), BF16),
                   jax.ShapeDtypeStruct((b, H_C, DK_C, DV_C), F32)],
        scratch_shapes=[pltpu.VMEM((nh, DV_C, DK_C), F32)],
        compiler_params=_cp(("parallel", "parallel", "arbitrary")), name="hgrn2",
    )(h3, h3, h3, h3, lb_logits, norm_g.reshape(1, H_C * DV_C), s0, lm, pm)
    return z, sn


def _merge_kernel(za_ref, zb_ref, zc_ref, ga_ref, gb_ref, gc_ref, x_ref, wa_ref, wb_ref, wc_ref, wo_ref,
                  g2_ref, x1_ref, xn_ref):
    ya = jnp.dot(za_ref[...], wa_ref[...], preferred_element_type=F32)
    yb = jnp.dot(zb_ref[...], wb_ref[...], preferred_element_type=F32)
    yc = jnp.dot(zc_ref[...], wc_ref[...], preferred_element_type=F32)
    y = _sigmoid(ga_ref[...]) * ya + _sigmoid(gb_ref[...]) * yb + _sigmoid(gc_ref[...]) * yc
    x1 = x_ref[...] + jnp.dot(y.astype(BF16), wo_ref[...], preferred_element_type=F32)
    x1_ref[...] = x1
    xn = x1 * lax.rsqrt(jnp.mean(x1 * x1, axis=-1, keepdims=True) + EPS) * g2_ref[...]
    xn_ref[...] = xn.astype(xn_ref.dtype)


def _merge(za, zb, zc, h2, x2d, wa, wb, wc, wo, g2):
    m, d = x2d.shape
    tm = _tile(m, 512)
    act = lambda: pl.BlockSpec((tm, d), lambda i: (i, 0))
    gate = lambda k: pl.BlockSpec((tm, d), lambda i: (i, OFF_MG // d + k))
    wgt = lambda: pl.BlockSpec((d, d), lambda i: (0, 0))
    return pl.pallas_call(
        _merge_kernel, grid=(m // tm,),
        in_specs=[act(), act(), act(), gate(0), gate(1), gate(2), act(), wgt(), wgt(), wgt(), wgt(),
                  pl.BlockSpec((1, d), lambda i: (0, 0))],
        out_specs=[act(), act()],
        out_shape=[jax.ShapeDtypeStruct((m, d), F32), jax.ShapeDtypeStruct((m, d), BF16)],
        compiler_params=_cp(("parallel",)), name="merge",
    )(za, zb, zc, h2, h2, h2, x2d, wa, wb, wc, wo, g2.reshape(1, d))


def _ffn_kernel(ug_ref, uv_ref, hg_ref, hv_ref, pg_ref, pv_ref, cw_ref, wd_ref, x1_ref, gn_ref,
                x2_ref, xn_ref, st_ref, fg_sc, fv_sc, *, ktaps, tt):
    t = pl.program_id(1)
    dff = ug_ref.shape[1]
    fg_sc[0:FFN_HALO, :] = jnp.where(t == 0, pg_ref[...], hg_ref[...])
    fv_sc[0:FFN_HALO, :] = jnp.where(t == 0, pv_ref[...], hv_ref[...])
    fg_sc[FFN_HALO:FFN_HALO + tt, :] = ug_ref[...]
    fv_sc[FFN_HALO:FFN_HALO + tt, :] = uv_ref[...]
    gate = jnp.zeros((tt, dff), F32)
    val = jnp.zeros((tt, dff), F32)
    for j in range(ktaps):
        off = FFN_HALO - (ktaps - 1) + j
        gate = gate + cw_ref[j:j + 1, 0:dff] * fg_sc[pl.ds(off, tt), :]
        val = val + cw_ref[j:j + 1, dff:2 * dff] * fv_sc[pl.ds(off, tt), :]
    act = (gate * _sigmoid(gate) * val).astype(BF16)
    x2 = x1_ref[...] + jnp.dot(act, wd_ref[...], preferred_element_type=F32)
    x2_ref[...] = x2
    xn = x2 * lax.rsqrt(jnp.mean(x2 * x2, axis=-1, keepdims=True) + EPS) * gn_ref[...]
    xn_ref[...] = xn.astype(xn_ref.dtype)
    st_ref[:, 0:dff] = fg_sc[tt:tt + FFN_HALO, :]
    st_ref[:, dff:2 * dff] = fv_sc[tt:tt + FFN_HALO, :]


def _ffn_tail(u3, prev, conv_w, w_down, x1_3, g_next, xn_dtype):
    b, t, two_dff = u3.shape
    dff = two_dff // 2
    d = x1_3.shape[2]
    ktaps = conv_w.shape[0]
    assert ktaps - 1 <= FFN_HALO
    tt = _tile(t, 256)
    nt = t // tt
    ratio = tt // FFN_HALO
    halo = lambda k: (lambda i, j: (i, jnp.maximum(j * ratio - 1, 0), k))
    x2, xn, st = pl.pallas_call(
        functools.partial(_ffn_kernel, ktaps=ktaps, tt=tt), grid=(b, nt),
        in_specs=[pl.BlockSpec((None, tt, dff), lambda i, j: (i, j, 0)),
                  pl.BlockSpec((None, tt, dff), lambda i, j: (i, j, 1)),
                  pl.BlockSpec((None, FFN_HALO, dff), halo(0)),
                  pl.BlockSpec((None, FFN_HALO, dff), halo(1)),
                  pl.BlockSpec((None, FFN_HALO, dff), lambda i, j: (i, 0, 0)),
                  pl.BlockSpec((None, FFN_HALO, dff), lambda i, j: (i, 0, 1)),
                  pl.BlockSpec((ktaps, two_dff), lambda i, j: (0, 0)),
                  pl.BlockSpec((dff, d), lambda i, j: (0, 0)),
                  pl.BlockSpec((None, tt, d), lambda i, j: (i, j, 0)),
                  pl.BlockSpec((1, d), lambda i, j: (0, 0))],
        out_specs=[pl.BlockSpec((None, tt, d), lambda i, j: (i, j, 0)),
                   pl.BlockSpec((None, tt, d), lambda i, j: (i, j, 0)),
                   pl.BlockSpec((None, FFN_HALO, two_dff), lambda i, j: (i, 0, 0))],
        out_shape=[jax.ShapeDtypeStruct((b, t, d), F32), jax.ShapeDtypeStruct((b, t, d), xn_dtype),
                   jax.ShapeDtypeStruct((b, FFN_HALO, two_dff), F32)],
        scratch_shapes=[pltpu.VMEM((FFN_HALO + tt, dff), F32), pltpu.VMEM((FFN_HALO + tt, dff), F32)],
        compiler_params=_cp(("parallel", "arbitrary")), name="ffn_tail",
    )(u3, u3, u3, u3, prev, prev, conv_w, w_down, x1_3, g_next.reshape(1, d))
    return x2, xn, st[:, FFN_HALO - (ktaps - 1):]


def _front_pad(x, rows):
    return jnp.pad(x, ((0, 0), (rows - x.shape[1], 0), (0, 0)))


def _layer(x3, xn2d, lw, layer, conv_a_prev, hg_prev, ffn_prev, nsa_fn, g_next, xn_dtype):
    b, t, d = x3.shape
    h2 = _matmul(xn2d, lw['w_in'], 2048, "in_proj")
    h3 = h2.reshape(b, t, N_PACK)
    gw = KVH * HD
    kv_rows = h3[:, :, OFF_KV:OFF_KV + 4 * gw].reshape(b, t, 4, KVH, HD)
    wrows = min(WINDOW, t)
    kv_win = h3[:, t - wrows:, OFF_KV + 4 * gw:OFF_KV + 6 * gw].reshape(b, wrows, 2, KVH, HD)

    za, conv_a_new = _conformer(h3, _front_pad(conv_a_prev, CONV_HALO), lw['conv_a_w'], lw['conv_a_b'],
                                lw['ln_a_g'], lw['ln_a_b'])
    zb = nsa_fn(h3)
    zc, hg_new = _hgrn(h3, hg_prev, lw['lb_logits'], lw['hg_norm_g'], layer)
    x1, xn2 = _merge(za.reshape(b * t, d), zb.reshape(b * t, d), zc.reshape(b * t, d), h2,
                     x3.reshape(b * t, d), lw['w_a_out'], lw['w_b_out'], lw['w_c_out'], lw['w_out'],
                     lw['norm2_g'])
    dff2 = lw['w_up'].shape[1]
    u2 = _matmul(xn2, lw['w_up'], dff2 // 2, "up_proj")
    x2, xn_next, ffn_new = _ffn_tail(u2.reshape(b, t, dff2), _front_pad(ffn_prev, FFN_HALO), lw['conv_f_w'],
                                     lw['w_down'], x1.reshape(b, t, d), g_next, xn_dtype)
    return x2, xn_next, kv_rows, kv_win, conv_a_new, hg_new, ffn_new


def kernel(x_prompt, x_sample, cache_nsa_kv, page_table, state_win_kv, state_conv_a, state_hgrn,
           state_ffn_conv, norm1_g, w_in, conv_a_w, conv_a_b, ln_a_g, ln_a_b, w_a_out, cmp_pe, cmp_w,
           w_b_out, hg_lb_logits, hg_norm_g, w_c_out, w_out, norm2_g, w_up, conv_f_w, w_down, final_g):
    depth = w_in.shape[0]
    bp, tp, d = x_prompt.shape
    bs, ts, _ = x_sample.shape
    assert d == 1024 and w_in.shape[2] == sum(IN_WIDTHS)
    cache_t = jnp.transpose(cache_nsa_kv, (0, 1, 3, 4, 5, 2))
    ka = conv_a_w.shape[1]
    kf = conv_f_w.shape[1]
    dff2 = w_up.shape[2]

    xp, xs = x_prompt, x_sample
    xnp = _rmsnorm(xp.reshape(bp * tp, d), norm1_g[0], BF16)
    xns = _rmsnorm(xs.reshape(bs * ts, d), norm1_g[0], BF16)
    outs = [[] for _ in range(10)]
    for l in range(depth):
        lw = {'w_in': _pack_w_in(w_in[l]), 'conv_a_w': conv_a_w[l], 'conv_a_b': conv_a_b[l],
              'ln_a_g': ln_a_g[l], 'ln_a_b': ln_a_b[l], 'w_a_out': w_a_out[l].astype(BF16),
              'w_b_out': w_b_out[l].astype(BF16), 'lb_logits': hg_lb_logits, 'hg_norm_g': hg_norm_g[l],
              'w_c_out': w_c_out[l].astype(BF16), 'w_out': w_out[l].astype(BF16), 'norm2_g': norm2_g[l],
              'w_up': w_up[l].astype(BF16), 'conv_f_w': conv_f_w[l], 'w_down': w_down[l].astype(BF16)}
        last = l == depth - 1
        g_next = final_g if last else norm1_g[l + 1]
        xn_dtype = F32 if last else BF16
        bd = _blockdiag_cmp_w(cmp_w[l])
        pe_t = _tile_pe(cmp_pe[l])
        nsa_p = functools.partial(_nsa_prompt, pe_t=pe_t, bd=bd)
        nsa_s = functools.partial(_nsa_sample, cache_t=cache_t, page_table=page_table, layer=l,
                                  win_prev=state_win_kv[l], pe=cmp_pe[l], bd=bd)
        xp, xnp, kv_p, win_p, ca_p, hg_p, ff_p = _layer(
            xp, xnp, lw, l, jnp.zeros((bp, ka - 1, W_A), F32), jnp.zeros((bp, H_C, DK_C, DV_C), F32),
            jnp.zeros((bp, kf - 1, dff2), F32), nsa_p, g_next, xn_dtype)
        xs, xns, kv_s, win_s, ca_s, hg_s, ff_s = _layer(
            xs, xns, lw, l, state_conv_a[l], state_hgrn[l], state_ffn_conv[l], nsa_s, g_next, xn_dtype)
        xnp = xnp.reshape(bp * tp, d)
        xns = xns.reshape(bs * ts, d)
        win_s_all = jnp.concatenate([state_win_kv[l], win_s], axis=1)
        for lst, v in zip(outs, (kv_p, kv_s, win_p, win_s_all[:, ts:], ca_p, ca_s, hg_p, hg_s, ff_p, ff_s)):
            lst.append(v)
    return (xnp.reshape(bp, tp, d), xns.reshape(bs, ts, d)) + tuple(jnp.stack(v) for v in outs)
```
